```python
import jax, jax.numpy as jnp
from jax import lax
import numpy as np

D_MODEL = 2048
BATCH = 2
SEQ = 4096
DEPTH = 1

MIX_WIDTH = D_MODEL
GLA_HEADS = 4
GLA_VALUE_DIM = MIX_WIDTH // 2
GLA_KEY_DIM = GLA_VALUE_DIM // 2
GLA_HEAD_K = GLA_KEY_DIM // GLA_HEADS
GLA_HEAD_V = GLA_VALUE_DIM // GLA_HEADS
GLA_GATE_RANK = 16
GLA_GATE_NORMALIZER = 16.0
HG_WIDTH = MIX_WIDTH - GLA_VALUE_DIM
HG_EXPAND = 128
HG_HEADS = HG_WIDTH // HG_EXPAND
HG_HEAD_V = HG_WIDTH // HG_HEADS
HG_FORGET_DIM = HG_HEADS * HG_EXPAND
IN_WIDTH = 2 * GLA_KEY_DIM + 2 * GLA_VALUE_DIM + GLA_GATE_RANK + 2 * HG_FORGET_DIM + 2 * HG_WIDTH
FFN_HIDDEN = -(-8 * D_MODEL // (3 * 256)) * 256
CHUNK = 64
NORM_EPS = 1e-6

kernel_name = "hymba_gla_hgrn2_adaln_block"


def rms_norm(x, w):
    xf = x.astype(jnp.float32)
    y = xf * lax.rsqrt(jnp.mean(xf * xf, axis=-1, keepdims=True) + NORM_EPS)
    return (y * w.astype(jnp.float32)).astype(x.dtype)


def chunk_gated_linear_attention(q, k, v, log_g, scale):
    B, T, H, Dk = q.shape
    Dv = v.shape[-1]
    N = T // CHUNK

    def to_chunks(a):
        return a.astype(jnp.float32).reshape(B, N, CHUNK, H, a.shape[-1]).transpose(1, 0, 3, 2, 4)

    qc, kc, vc, gc = to_chunks(q * scale), to_chunks(k), to_chunks(v), to_chunks(log_g)
    causal = jnp.tril(jnp.ones((CHUNK, CHUNK), dtype=bool))[:, :, None]

    def step(S, inp):
        qi, ki, vi, gi = inp
        b = jnp.cumsum(gi, axis=-2)
        b_last = b[..., -1:, :]
        o_inter = jnp.einsum('bhcd,bhde->bhce', qi * jnp.exp(b), S)
        diff = b[..., :, None, :] - b[..., None, :, :]
        decay = jnp.exp(jnp.where(causal, diff, -jnp.inf))
        scores = jnp.einsum('bhid,bhjd,bhijd->bhij', qi, ki, decay)
        o = o_inter + jnp.einsum('bhij,bhje->bhie', scores, vi)
        S_new = jnp.swapaxes(jnp.exp(b_last), -1, -2) * S + jnp.einsum(
            'bhcd,bhce->bhde', ki * jnp.exp(b_last - b), vi)
        return S_new, o

    S0 = jnp.zeros((B, H, Dk, Dv), jnp.float32)
    _, o = lax.scan(step, S0, (qc, kc, vc, gc))
    return o.transpose(1, 0, 3, 2, 4).reshape(B, T, H, Dv).astype(v.dtype)


def split_in_projection(p):
    sizes = (GLA_KEY_DIM, GLA_KEY_DIM, GLA_VALUE_DIM, GLA_VALUE_DIM, GLA_GATE_RANK,
             HG_FORGET_DIM, HG_FORGET_DIM, HG_WIDTH, HG_WIDTH)
    idx = np.cumsum(sizes)[:-1].tolist()
    return jnp.split(p, idx, axis=-1)


def hybrid_mixer(h, w_in, w_gla_gate, b_gla_gate, gla_norm_w, lb, hg_norm_w, w_out):
    B, T, _ = h.shape
    p = h @ w_in
    gla_q, gla_k, gla_v, gla_g, gla_lr, hg_q, hg_f, hg_i, hg_g = split_in_projection(p)

    log_alpha = jax.nn.log_sigmoid((gla_lr @ w_gla_gate + b_gla_gate).astype(jnp.float32)) / GLA_GATE_NORMALIZER
    o_gla = chunk_gated_linear_attention(
        gla_q.reshape(B, T, GLA_HEADS, GLA_HEAD_K),
        gla_k.reshape(B, T, GLA_HEADS, GLA_HEAD_K),
        gla_v.reshape(B, T, GLA_HEADS, GLA_HEAD_V),
        log_alpha.reshape(B, T, GLA_HEADS, GLA_HEAD_K),
        GLA_HEAD_K ** -0.5)
    o_gla = rms_norm(o_gla, gla_norm_w).reshape(B, T, GLA_VALUE_DIM) * jax.nn.silu(gla_g)

    f = lb + (1.0 - lb) * jax.nn.sigmoid(hg_f.astype(jnp.float32))
    o_hg = chunk_gated_linear_attention(
        jax.nn.silu(hg_q).reshape(B, T, HG_HEADS, HG_EXPAND),
        (1.0 - f).reshape(B, T, HG_HEADS, HG_EXPAND),
        hg_i.reshape(B, T, HG_HEADS, HG_HEAD_V),
        jnp.log(f).reshape(B, T, HG_HEADS, HG_EXPAND),
        1.0)
    o_hg = rms_norm(o_hg, hg_norm_w).reshape(B, T, HG_WIDTH) * jax.nn.silu(hg_g)

    return jnp.concatenate([o_gla, o_hg], axis=-1) @ w_out


def swiglu(h, w_ffn_in, w_ffn_out):
    a, u = jnp.split(h @ w_ffn_in, 2, axis=-1)
    return (jax.nn.silu(a) * u) @ w_ffn_out


def setup_inputs(seed: int = 0) -> dict:
    key = jax.random.key(seed)
    ks = jax.random.split(key, 18)

    def nrm(k, shape, s):
        return jax.random.normal(k, shape, jnp.float32) * s

    return {
        "x": nrm(ks[0], (BATCH, SEQ, D_MODEL), 1.0),
        "c": nrm(ks[1], (BATCH, D_MODEL), 1.0),
        "w_ada": nrm(ks[2], (DEPTH, D_MODEL, 6 * D_MODEL), 0.5 * D_MODEL ** -0.5),
        "b_ada": nrm(ks[3], (DEPTH, 6 * D_MODEL), 0.01),
        "norm_mix_w": 1.0 + nrm(ks[4], (DEPTH, D_MODEL), 0.02),
        "w_in": nrm(ks[5], (DEPTH, D_MODEL, IN_WIDTH), D_MODEL ** -0.5),
        "w_gla_gate": nrm(ks[6], (DEPTH, GLA_GATE_RANK, GLA_KEY_DIM), GLA_GATE_RANK ** -0.5),
        "b_gla_gate": nrm(ks[7], (DEPTH, GLA_KEY_DIM), 0.1),
        "gla_norm_w": 1.0 + nrm(ks[8], (DEPTH, GLA_HEAD_V), 0.02),
        "hg_lower_bound_logits": nrm(ks[9], (DEPTH + 1, HG_FORGET_DIM), 0.1),
        "hg_norm_w": 1.0 + nrm(ks[10], (DEPTH, HG_HEAD_V), 0.02),
        "w_out": nrm(ks[11], (DEPTH, MIX_WIDTH, D_MODEL), MIX_WIDTH ** -0.5),
        "norm_ffn_w": 1.0 + nrm(ks[12], (DEPTH, D_MODEL), 0.02),
        "w_ffn_in": nrm(ks[13], (DEPTH, D_MODEL, 2 * FFN_HIDDEN), D_MODEL ** -0.5),
        "w_ffn_out": nrm(ks[14], (DEPTH, FFN_HIDDEN, D_MODEL), FFN_HIDDEN ** -0.5),
        "final_norm_w": 1.0 + nrm(ks[15], (D_MODEL,), 0.02),
    }


def reference(x, c, w_ada, b_ada, norm_mix_w, w_in, w_gla_gate, b_gla_gate, gla_norm_w,
              hg_lower_bound_logits, hg_norm_w, w_out, norm_ffn_w, w_ffn_in, w_ffn_out,
              final_norm_w):
    lb_all = jnp.cumsum(jax.nn.softmax(hg_lower_bound_logits.astype(jnp.float32), axis=0), axis=0)[:DEPTH]
    c_act = jax.nn.silu(c)
    for l in range(DEPTH):
        mod = c_act @ w_ada[l] + b_ada[l]
        shift_m, scale_m, gate_m, shift_f, scale_f, gate_f = jnp.split(mod[:, None, :], 6, axis=-1)
        h = rms_norm(x, norm_mix_w[l]) * (1.0 + scale_m) + shift_m
        y = hybrid_mixer(h, w_in[l], w_gla_gate[l], b_gla_gate[l], gla_norm_w[l], lb_all[l],
                         hg_norm_w[l], w_out[l])
        x = x + gate_m * y
        h = rms_norm(x, norm_ffn_w[l]) * (1.0 + scale_f) + shift_f
        x = x + gate_f * swiglu(h, w_ffn_in[l], w_ffn_out[l])
    return rms_norm(x, final_norm_w)
```

```python
import functools

import numpy as np
import jax
import jax.numpy as jnp
from jax import lax
from jax.experimental import pallas as pl
from jax.experimental.pallas import tpu as pltpu

F32 = jnp.float32
BF16 = jnp.bfloat16

NORM_EPS = 1e-6
GLA_HEADS = 4
GLA_GATE_RANK = 16
GLA_GATE_NORMALIZER = 16.0
HG_EXPAND = 128
LANES = 128
MIX_CHUNK = 128
VMEM_LIMIT = 56 * 1024 * 1024


def _mm(a, b):
    return jnp.dot(a, b, preferred_element_type=F32)


def _mm_nt(a, b):
    return lax.dot_general(a, b, (((1,), (1,)), ((), ())), preferred_element_type=F32)


def _mm_tn(a, b):
    return lax.dot_general(a, b, (((0,), (0,)), ((), ())), preferred_element_type=F32)


def _silu(x):
    return x * jax.nn.sigmoid(x)


def _rms(x, w):
    var = jnp.mean(x * x, axis=-1, keepdims=True)
    return x * lax.rsqrt(var + NORM_EPS) * w


def _ada_kernel(cb_ref, w_ref, b_ref, o_ref):
    nb = cb_ref.shape[0]
    tn = w_ref.shape[1]
    acts = [_silu(cb_ref[b]) for b in range(nb)]
    for n in range(tn // LANES):
        w = w_ref[:, n * LANES:(n + 1) * LANES]
        for b in range(nb):
            o_ref[b:b + 1, n * LANES:(n + 1) * LANES] = (
                jnp.sum(w * acts[b], axis=0, keepdims=True)
                + b_ref[:, n * LANES:(n + 1) * LANES])


def _ada(c, w_ada, b_ada, tn=512):
    nb, d = c.shape
    n = w_ada.shape[1]
    cb = jnp.broadcast_to(c[:, :, None], (nb, d, LANES))
    return pl.pallas_call(
        _ada_kernel,
        grid=(n // tn,),
        in_specs=[pl.BlockSpec((nb, d, LANES), lambda j: (0, 0, 0)),
                  pl.BlockSpec((d, tn), lambda j: (0, j)),
                  pl.BlockSpec((1, tn), lambda j: (0, j))],
        out_specs=pl.BlockSpec((nb, tn), lambda j: (0, j)),
        out_shape=jax.ShapeDtypeStruct((nb, n), F32),
        compiler_params=pltpu.CompilerParams(
            dimension_semantics=("arbitrary",), vmem_limit_bytes=VMEM_LIMIT),
        name="ada",
    )(cb, w_ada, b_ada.reshape(1, n))


def _inproj_kernel(x_ref, nw_ref, sc_ref, sh_ref, wm_ref, wg_ref, om_ref, og_ref, h_ref, *, nj):
    j = pl.program_id(1)

    @pl.when(j == 0)
    def _():
        h = _rms(x_ref[...], nw_ref[...]) * (1.0 + sc_ref[0]) + sh_ref[0]
        h_ref[...] = h.astype(BF16)

    @pl.when(j < nj - 1)
    def _():
        om_ref[...] = _mm(h_ref[...], wm_ref[...]).astype(BF16)

    @pl.when(j == nj - 1)
    def _():
        og_ref[...] = _mm(h_ref[...], wg_ref[...])


def _inproj(x2d, norm_w, scale, shift, wm, wg, seq, tm=512, tn=1024):
    m, d = x2d.shape
    nm = wm.shape[1]
    ng = wg.shape[1]
    nj = nm // tn + 1
    tpb = seq // tm
    return pl.pallas_call(
        functools.partial(_inproj_kernel, nj=nj),
        grid=(m // tm, nj),
        in_specs=[pl.BlockSpec((tm, d), lambda i, j: (i, 0)),
                  pl.BlockSpec((1, d), lambda i, j: (0, 0)),
                  pl.BlockSpec((1, 1, d), lambda i, j: (i // tpb, 0, 0)),
                  pl.BlockSpec((1, 1, d), lambda i, j: (i // tpb, 0, 0)),
                  pl.BlockSpec((d, tn), lambda i, j: (0, jnp.minimum(j, nj - 2))),
                  pl.BlockSpec((d, ng), lambda i, j: (0, 0))],
        out_specs=[pl.BlockSpec((tm, tn), lambda i, j: (i, jnp.minimum(j, nj - 2))),
                   pl.BlockSpec((tm, ng), lambda i, j: (i, 0))],
        out_shape=[jax.ShapeDtypeStruct((m, nm), BF16),
                   jax.ShapeDtypeStruct((m, ng), F32)],
        scratch_shapes=[pltpu.VMEM((tm, d), BF16)],
        compiler_params=pltpu.CompilerParams(
            dimension_semantics=("arbitrary", "arbitrary"), vmem_limit_bytes=VMEM_LIMIT),
        name="inproj",
    )(x2d, norm_w.reshape(1, d), scale, shift, wm, wg)


def _level_matrix(c):
    t = np.arange(c)[:, None]
    s = np.arange(c)[None, :]
    blocks = [s <= t, s > t]
    h = c // 2
    while h >= 1:
        mid = (t // (2 * h)) * (2 * h) + h - 1
        second = (t % (2 * h)) >= h
        blocks.append(np.where(second, (s > mid) & (s <= t), (s > t) & (s <= mid)))
        h //= 2
    return np.concatenate(blocks, axis=0).astype(np.float32)


def _level_masks(c):
    i = np.arange(c)[:, None]
    j = np.arange(c)[None, :]
    masks = [i == j]
    h = c // 2
    while h >= 1:
        same = (i // (2 * h)) == (j // (2 * h))
        masks.append(same & ((i % (2 * h)) >= h) & ((j % (2 * h)) < h))
        h //= 2
    m = np.stack(masks).astype(np.float32)
    return np.concatenate([m, m], axis=2)


def _mixer_kernel(pm_ref, pg_ref, wlev_ref, mask_ref, wgate_ref, bgate_ref, lbl_ref,
                  gnw_ref, hnw_ref, o_ref, sg_ref, sh_ref, *, gla_scale):
    c = pm_ref.shape[0]
    nlev = mask_ref.shape[0] - 1
    kd = HG_EXPAND
    n_gla = sg_ref.shape[0]
    n_hg = sh_ref.shape[0]
    gla_k = n_gla * kd
    gla_v = n_gla * sg_ref.shape[1]
    hg_w = n_hg * kd
    o_gq, o_gk, o_gv, o_gg = 0, gla_k, 2 * gla_k, 2 * gla_k + gla_v
    o_hq = o_gg + gla_v
    o_hi = o_hq + hg_w
    o_hg = o_hi + hg_w

    @pl.when(pl.program_id(1) == 0)
    def _():
        sg_ref[...] = jnp.zeros_like(sg_ref)
        sh_ref[...] = jnp.zeros_like(sh_ref)

    lr = pg_ref[:, hg_w:hg_w + LANES].astype(BF16)
    z = _mm(lr, wgate_ref[...]) + bgate_ref[...]
    g_gla = (jnp.minimum(z, 0.0) - jnp.log1p(jnp.exp(-jnp.abs(z)))) * (1.0 / GLA_GATE_NORMALIZER)
    lbl = lbl_ref[...]
    lbe = jnp.exp(lbl - jnp.max(lbl, axis=0, keepdims=True))
    lb = lbe[0:1, :] / jnp.sum(lbe, axis=0, keepdims=True)
    f = lb + (1.0 - lb) * jax.nn.sigmoid(pg_ref[:, 0:hg_w])
    g_hg = jnp.log(f)
    k_hg = 1.0 - f

    wlev = wlev_ref[...]

    def pair(g2, q2, k2, vs, gates, st_ref, heads, nw, out_off, dv):
        g_hi = g2.astype(BF16)
        g_lo = (g2 - g_hi.astype(F32)).astype(BF16)
        ex = jnp.exp(_mm(wlev, g_hi) + _mm(wlev, g_lo))
        e_b = ex[0:c]
        e_r = ex[c:2 * c]
        zero = jnp.zeros((c, kd), BF16)

        def blockdiag(kk):
            return jnp.concatenate(
                [jnp.concatenate([kk[:, :kd], zero], axis=1),
                 jnp.concatenate([zero, kk[:, kd:]], axis=1)], axis=0)

        s = mask_ref[0] * _mm_nt(q2.astype(BF16), blockdiag(k2.astype(BF16)))
        for lv in range(nlev):
            e = ex[(2 + lv) * c:(3 + lv) * c]
            s = s + mask_ref[1 + lv] * _mm_nt((q2 * e).astype(BF16),
                                              blockdiag((k2 * e).astype(BF16)))
        qe = (q2 * e_b).astype(BF16)
        ke = (k2 * e_r).astype(BF16)
        for i, hd in enumerate(heads):
            sl = slice(i * kd, (i + 1) * kd)
            st = st_ref[hd]
            o = _mm(s[:, i * c:(i + 1) * c].astype(BF16), vs[i]) + _mm_nt(qe[:, sl], st.astype(BF16))
            st_ref[hd] = st * e_b[c - 1:c, sl] + _mm_tn(vs[i], ke[:, sl])
            on = _rms(o, nw) * _silu(gates[i].astype(F32))
            o_ref[:, out_off + hd * dv:out_off + (hd + 1) * dv] = on.astype(o_ref.dtype)

    dv_g = sg_ref.shape[1]
    for p in range(n_gla // 2):
        lo = 2 * p * kd
        heads = (2 * p, 2 * p + 1)
        pair(g_gla[:, lo:lo + 2 * kd],
             pm_ref[:, o_gq + lo:o_gq + lo + 2 * kd].astype(F32) * gla_scale,
             pm_ref[:, o_gk + lo:o_gk + lo + 2 * kd].astype(F32),
             [pm_ref[:, o_gv + hd * dv_g:o_gv + (hd + 1) * dv_g] for hd in heads],
             [pm_ref[:, o_gg + hd * dv_g:o_gg + (hd + 1) * dv_g] for hd in heads],
             sg_ref, heads, gnw_ref[...], 0, dv_g)
    dv_h = sh_ref.shape[1]
    for p in range(n_hg // 2):
        lo = 2 * p * kd
        heads = (2 * p, 2 * p + 1)
        pair(g_hg[:, lo:lo + 2 * kd],
             _silu(pm_ref[:, o_hq + lo:o_hq + lo + 2 * kd].astype(F32)),
             k_hg[:, lo:lo + 2 * kd],
             [pm_ref[:, o_hi + hd * dv_h:o_hi + (hd + 1) * dv_h] for hd in heads],
             [pm_ref[:, o_hg + hd * dv_h:o_hg + (hd + 1) * dv_h] for hd in heads],
             sh_ref, heads, hnw_ref[...], gla_v, dv_h)


def _mixer(pm, pg, w_gate_pad, b_gate, lb_logits, gla_norm_w, hg_norm_w, batch, seq):
    m = pm.shape[0]
    c = MIX_CHUNK
    steps = seq // c
    wlev = jnp.asarray(_level_matrix(c), BF16)
    masks = jnp.asarray(_level_masks(c), F32)
    dv_g = gla_norm_w.shape[0]
    dv_h = hg_norm_w.shape[0]
    n_hg = lb_logits.shape[1] // HG_EXPAND
    width = GLA_HEADS * dv_g + n_hg * dv_h
    const2 = lambda b, t: (0, 0)
    return pl.pallas_call(
        functools.partial(_mixer_kernel, gla_scale=float(HG_EXPAND) ** -0.5),
        grid=(batch, steps),
        in_specs=[pl.BlockSpec((c, pm.shape[1]), lambda b, t: (b * steps + t, 0)),
                  pl.BlockSpec((c, pg.shape[1]), lambda b, t: (b * steps + t, 0)),
                  pl.BlockSpec(wlev.shape, const2),
                  pl.BlockSpec(masks.shape, lambda b, t: (0, 0, 0)),
                  pl.BlockSpec(w_gate_pad.shape, const2),
                  pl.BlockSpec((1, b_gate.shape[0]), const2),
                  pl.BlockSpec(lb_logits.shape, const2),
                  pl.BlockSpec((1, dv_g), const2),
                  pl.BlockSpec((1, dv_h), const2)],
        out_specs=pl.BlockSpec((c, width), lambda b, t: (b * steps + t, 0)),
        out_shape=jax.ShapeDtypeStruct((m, width), BF16),
        scratch_shapes=[pltpu.VMEM((GLA_HEADS, dv_g, HG_EXPAND), F32),
                        pltpu.VMEM((n_hg, dv_h, HG_EXPAND), F32)],
        compiler_params=pltpu.CompilerParams(
            dimension_semantics=("arbitrary", "arbitrary"), vmem_limit_bytes=VMEM_LIMIT),
        name="mixer",
    )(pm, pg, wlev, masks, w_gate_pad, b_gate.reshape(1, -1), lb_logits,
      gla_norm_w.reshape(1, -1), hg_norm_w.reshape(1, -1))


def _outproj_kernel(o_ref, w_ref, x_ref, gate_ref, x1_ref):
    x1_ref[...] = x_ref[...] + gate_ref[0] * _mm(o_ref[...], w_ref[...])


def _outproj(o, w_out, x2d, gate, seq, tm=512):
    m, d = x2d.shape
    tpb = seq // tm
    return pl.pallas_call(
        _outproj_kernel,
        grid=(m // tm,),
        in_specs=[pl.BlockSpec((tm, o.shape[1]), lambda i: (i, 0)),
                  pl.BlockSpec(w_out.shape, lambda i: (0, 0)),
                  pl.BlockSpec((tm, d), lambda i: (i, 0)),
                  pl.BlockSpec((1, 1, d), lambda i: (i // tpb, 0, 0))],
        out_specs=pl.BlockSpec((tm, d), lambda i: (i, 0)),
        out_shape=jax.ShapeDtypeStruct((m, d), F32),
        compiler_params=pltpu.CompilerParams(
            dimension_semantics=("arbitrary",), vmem_limit_bytes=VMEM_LIMIT),
        name="outproj",
    )(o, w_out, x2d, gate)


def _ffn_kernel(x_ref, nw_ref, sc_ref, sh_ref, gate_ref, fw_ref, wa_ref, wu_ref, wo_ref,
                out_ref, h_ref, acc_ref, *, nj):
    j = pl.program_id(1)

    @pl.when(j == 0)
    def _():
        h = _rms(x_ref[...], nw_ref[...]) * (1.0 + sc_ref[0]) + sh_ref[0]
        h_ref[...] = h.astype(BF16)
        acc_ref[...] = jnp.zeros_like(acc_ref)

    h = h_ref[...]
    act = _silu(_mm(h, wa_ref[...])) * _mm(h, wu_ref[...])
    acc_ref[...] += _mm(act.astype(BF16), wo_ref[...])

    @pl.when(j == nj - 1)
    def _():
        out_ref[...] = _rms(x_ref[...] + gate_ref[0] * acc_ref[...], fw_ref[...])


def _ffn(x1, norm_w, scale, shift, gate, final_w, w_in, w_out, seq, tm=512, tf=512):
    m, d = x1.shape
    hidden = w_out.shape[0]
    nj = hidden // tf
    tpb = seq // tm
    vec = pl.BlockSpec((1, d), lambda i, j: (0, 0))
    mod = pl.BlockSpec((1, 1, d), lambda i, j: (i // tpb, 0, 0))
    return pl.pallas_call(
        functools.partial(_ffn_kernel, nj=nj),
        grid=(m // tm, nj),
        in_specs=[pl.BlockSpec((tm, d), lambda i, j: (i, 0)),
                  vec, mod, mod, mod, vec,
                  pl.BlockSpec((d, tf), lambda i, j: (0, j)),
                  pl.BlockSpec((d, tf), lambda i, j: (0, j + nj)),
                  pl.BlockSpec((tf, d), lambda i, j: (j, 0))],
        out_specs=pl.BlockSpec((tm, d), lambda i, j: (i, 0)),
        out_shape=jax.ShapeDtypeStruct((m, d), F32),
        scratch_shapes=[pltpu.VMEM((tm, d), BF16), pltpu.VMEM((tm, d), F32)],
        compiler_params=pltpu.CompilerParams(
            dimension_semantics=("arbitrary", "arbitrary"), vmem_limit_bytes=VMEM_LIMIT),
        name="ffn",
    )(x1, norm_w.reshape(1, d), scale, shift, gate, final_w.reshape(1, d), w_in, w_in, w_out)


def kernel(x, c, w_ada, b_ada, norm_mix_w, w_in, w_gla_gate, b_gla_gate, gla_norm_w,
           hg_lower_bound_logits, hg_norm_w, w_out, norm_ffn_w, w_ffn_in, w_ffn_out,
           final_norm_w):
    batch, seq, d = x.shape
    assert w_ada.shape[0] == 1, "single-layer block"
    dv_g = gla_norm_w.shape[1]
    dv_h = hg_norm_w.shape[1]
    gla_v = GLA_HEADS * dv_g
    gla_k = GLA_HEADS * HG_EXPAND
    hg_f = hg_lower_bound_logits.shape[1]
    hg_w = (hg_f // HG_EXPAND) * dv_h
    x2d = x.reshape(batch * seq, d)

    mod = _ada(c, w_ada[0], b_ada[0])
    shift_m, scale_m, gate_m, shift_f, scale_f, gate_f = [
        mod[:, i * d:(i + 1) * d].reshape(batch, 1, d) for i in range(6)]

    wi = w_in[0]
    offs = np.cumsum([0, gla_k, gla_k, gla_v, gla_v, GLA_GATE_RANK, hg_f, hg_f, hg_w, hg_w])
    col = lambda i: wi[:, offs[i]:offs[i + 1]]
    wm = jnp.concatenate([col(0), col(1), col(2), col(3), col(5), col(7), col(8)], axis=1).astype(BF16)
    wg = jnp.concatenate(
        [col(6), col(4), jnp.zeros((d, LANES - GLA_GATE_RANK), wi.dtype)], axis=1).astype(BF16)
    w_gate_pad = jnp.concatenate(
        [w_gla_gate[0], jnp.zeros((LANES - GLA_GATE_RANK, gla_k), w_gla_gate.dtype)], axis=0).astype(BF16)

    pm, pg = _inproj(x2d, norm_mix_w[0], scale_m, shift_m, wm, wg, seq)
    o = _mixer(pm, pg, w_gate_pad, b_gla_gate[0], hg_lower_bound_logits, gla_norm_w[0],
               hg_norm_w[0], batch, seq)
    x1 = _outproj(o, w_out[0].astype(BF16), x2d, gate_m, seq)
    out = _ffn(x1, norm_ffn_w[0], scale_f, shift_f, gate_f, final_norm_w,
               w_ffn_in[0].astype(BF16), w_ffn_out[0].astype(BF16), seq)
    return out.reshape(batch, seq, d)
```

```python
import functools

import numpy as np
import jax
import jax.numpy as jnp
from jax import lax
from jax.experimental import pallas as pl
from jax.experimental.pallas import tpu as pltpu

F32 = jnp.float32
BF16 = jnp.bfloat16

NORM_EPS = 1e-6
LOG2_E = 1.4426950408889634
GLA_HEADS = 4
GLA_GATE_RANK = 16
GLA_GATE_NORMALIZER = 16.0
HG_EXPAND = 128
LANES = 128
SUBLANES = 8
MIX_CHUNK = 128
VMEM_LIMIT = 56 * 1024 * 1024


def _mm(a, b):
    return jnp.dot(a, b, preferred_element_type=F32)


def _mm_nt(a, b):
    return lax.dot_general(a, b, (((1,), (1,)), ((), ())), preferred_element_type=F32)


def _mm_tn(a, b):
    return lax.dot_general(a, b, (((0,), (0,)), ((), ())), preferred_element_type=F32)


def _silu(x):
    return x * jax.nn.sigmoid(x)


def _rms(x, w):
    var = jnp.mean(x * x, axis=-1, keepdims=True)
    return x * lax.rsqrt(var + NORM_EPS) * w


def _ada_kernel(cb_ref, w_ref, b_ref, o_ref):
    nb = cb_ref.shape[0]
    tn = w_ref.shape[1]
    acts = [_silu(cb_ref[b]) for b in range(nb)]
    for n in range(tn // LANES):
        w = w_ref[:, n * LANES:(n + 1) * LANES]
        for b in range(nb):
            o_ref[b:b + 1, n * LANES:(n + 1) * LANES] = (
                jnp.sum(w * acts[b], axis=0, keepdims=True)
                + b_ref[:, n * LANES:(n + 1) * LANES])


def _ada(c, w_ada, b_ada, tn=512):
    nb, d = c.shape
    n = w_ada.shape[1]
    cb = jnp.broadcast_to(c[:, :, None], (nb, d, LANES))
    return pl.pallas_call(
        _ada_kernel,
        grid=(n // tn,),
        in_specs=[pl.BlockSpec((nb, d, LANES), lambda j: (0, 0, 0)),
                  pl.BlockSpec((d, tn), lambda j: (0, j)),
                  pl.BlockSpec((1, tn), lambda j: (0, j))],
        out_specs=pl.BlockSpec((nb, tn), lambda j: (0, j)),
        out_shape=jax.ShapeDtypeStruct((nb, n), F32),
        compiler_params=pltpu.CompilerParams(
            dimension_semantics=("arbitrary",), vmem_limit_bytes=VMEM_LIMIT),
        name="ada",
    )(cb, w_ada, b_ada.reshape(1, n))


def _inproj_kernel(x_ref, nw_ref, sc_ref, sh_ref, w_ref, o_ref, h_ref):
    @pl.when(pl.program_id(1) == 0)
    def _():
        h = _rms(x_ref[...], nw_ref[...]) * (1.0 + sc_ref[0]) + sh_ref[0]
        h_ref[...] = h.astype(BF16)

    o_ref[...] = _mm(h_ref[...], w_ref[...]).astype(o_ref.dtype)


def _inproj(x2d, norm_w, scale, shift, w, seq, tm=512, n_tiles=3):
    m, d = x2d.shape
    n = w.shape[1]
    tn = n // n_tiles
    tpb = seq // tm
    return pl.pallas_call(
        _inproj_kernel,
        grid=(m // tm, n_tiles),
        in_specs=[pl.BlockSpec((tm, d), lambda i, j: (i, 0)),
                  pl.BlockSpec((1, d), lambda i, j: (0, 0)),
                  pl.BlockSpec((1, 1, d), lambda i, j: (i // tpb, 0, 0)),
                  pl.BlockSpec((1, 1, d), lambda i, j: (i // tpb, 0, 0)),
                  pl.BlockSpec((d, tn), lambda i, j: (0, j))],
        out_specs=pl.BlockSpec((tm, tn), lambda i, j: (i, j)),
        out_shape=jax.ShapeDtypeStruct((m, n), BF16),
        scratch_shapes=[pltpu.VMEM((tm, d), BF16)],
        compiler_params=pltpu.CompilerParams(
            dimension_semantics=("arbitrary", "arbitrary"), vmem_limit_bytes=VMEM_LIMIT),
        name="inproj",
    )(x2d, norm_w.reshape(1, d), scale, shift, w)


def _small_levels(c):
    return [h for h in (SUBLANES // 2, SUBLANES // 4, SUBLANES // 8) if 1 <= h < c]


def _big_levels(c):
    out, h = [], c // 2
    while h >= SUBLANES:
        out.append(h)
        h //= 2
    return out


def _cumsum_matrix(c):
    t = np.arange(c)[:, None]
    s = np.arange(c)[None, :]
    blocks = [s <= t]
    for h in _small_levels(c):
        mid = (t // (2 * h)) * (2 * h) + h - 1
        second = (t % (2 * h)) >= h
        blocks.append(np.where(second, (s > mid) & (s <= t), (s > t) & (s <= mid)))
    return np.concatenate(blocks, axis=0).astype(np.float32)


def _level_masks(c):
    i = np.arange(c)[:, None]
    j = np.arange(c)[None, :]
    masks = [i == j]
    for h in _big_levels(c) + _small_levels(c):
        same = (i // (2 * h)) == (j // (2 * h))
        masks.append(same & ((i % (2 * h)) >= h) & ((j % (2 * h)) < h))
    m = np.stack(masks).astype(np.float32)
    return np.concatenate([m, m], axis=2)


def _mixer_kernel(p_ref, wcs_ref, mask_ref, wgate_ref, bgate_ref, lbl_ref,
                  gnw_ref, hnw_ref, o_ref, sg_ref, sh_ref, *, gla_scale):
    c = p_ref.shape[0]
    kd = HG_EXPAND
    n_gla, dv_g = sg_ref.shape[0], sg_ref.shape[1]
    n_hg, dv_h = sh_ref.shape[0], sh_ref.shape[1]
    gla_k, gla_v, hg_w = n_gla * kd, n_gla * dv_g, n_hg * kd
    o_gq, o_gk, o_gv, o_gg = 0, gla_k, 2 * gla_k, 2 * gla_k + gla_v
    o_lr = o_gg + gla_v
    o_hq = o_lr + LANES
    o_hf = o_hq + hg_w
    o_hi = o_hf + hg_w
    o_hg = o_hi + hg_w
    big = _big_levels(c)
    small = _small_levels(c)

    @pl.when(pl.program_id(1) == 0)
    def _():
        sg_ref[...] = jnp.zeros_like(sg_ref)
        sh_ref[...] = jnp.zeros_like(sh_ref)

    z = _mm(p_ref[:, o_lr:o_lr + LANES], wgate_ref[...]) + bgate_ref[...]
    g_gla = (jnp.minimum(z, 0.0) - jnp.log1p(jnp.exp(-jnp.abs(z)))) * (LOG2_E / GLA_GATE_NORMALIZER)
    lbl = lbl_ref[...]
    lbe = jnp.exp(lbl - jnp.max(lbl, axis=0, keepdims=True))
    lb = lbe[0:1, :] / jnp.sum(lbe, axis=0, keepdims=True)
    f = lb + (1.0 - lb) * jax.nn.sigmoid(p_ref[:, o_hf:o_hf + hg_w].astype(F32))
    g_hg = jnp.log2(f)
    k_hg = (1.0 - f).astype(BF16)

    wcs = wcs_ref[...]
    zero = jnp.zeros((c, kd), BF16)

    def blockdiag(kk):
        return jnp.concatenate(
            [jnp.concatenate([kk[:, :kd], zero], axis=1),
             jnp.concatenate([zero, kk[:, kd:]], axis=1)], axis=0)

    def pair(g2, qb, kb, vs, gates, st_ref, heads, nw, out_off, dv, q_scale):
        g_hi = g2.astype(BF16)
        g_lo = (g2 - g_hi.astype(F32)).astype(BF16)
        cs = _mm(wcs, g_hi)
        b = cs[0:c] + _mm(wcs[0:c], g_lo)
        exps = []
        for h in big:
            parts = []
            for s in range(0, c, 2 * h):
                mid = jnp.broadcast_to(b[s + h - 1:s + h, :], (h, 2 * kd))
                parts += [mid - b[s:s + h], b[s + h:s + 2 * h] - mid]
            exps.append(jnp.exp2(jnp.concatenate(parts, axis=0)))
        for i in range(len(small)):
            exps.append(jnp.exp2(cs[(1 + i) * c:(2 + i) * c]))
        e_b = jnp.exp2(b)
        e_r = jnp.exp2(jnp.broadcast_to(b[c - 1:c, :], (c, 2 * kd)) - b)

        s = jnp.where(mask_ref[0] > 0, _mm_nt(qb, blockdiag(kb)).astype(BF16), 0.0)
        for lv, e in enumerate(exps):
            eb = e.astype(BF16)
            s = jnp.where(mask_ref[1 + lv] > 0,
                          _mm_nt(qb * eb, blockdiag(kb * eb)).astype(BF16), s)
        qe = qb * e_b.astype(BF16)
        ke = kb * e_r.astype(BF16)
        for i, hd in enumerate(heads):
            sl = slice(i * kd, (i + 1) * kd)
            st = st_ref[hd]
            o = _mm(s[:, i * c:(i + 1) * c], vs[i]) + _mm_nt(qe[:, sl], st.astype(BF16))
            st_ref[hd] = st * e_b[c - 1:c, sl] + _mm_tn(vs[i], ke[:, sl])
            var = jnp.mean(o * o, axis=-1, keepdims=True)
            r = q_scale * lax.rsqrt((q_scale * q_scale) * var + NORM_EPS)
            on = o * r * nw * _silu(gates[i].astype(F32))
            o_ref[:, out_off + hd * dv:out_off + (hd + 1) * dv] = on.astype(o_ref.dtype)

    for p in range(n_gla // 2):
        lo = 2 * p * kd
        heads = (2 * p, 2 * p + 1)
        pair(g_gla[:, lo:lo + 2 * kd],
             p_ref[:, o_gq + lo:o_gq + lo + 2 * kd],
             p_ref[:, o_gk + lo:o_gk + lo + 2 * kd],
             [p_ref[:, o_gv + hd * dv_g:o_gv + (hd + 1) * dv_g] for hd in heads],
             [p_ref[:, o_gg + hd * dv_g:o_gg + (hd + 1) * dv_g] for hd in heads],
             sg_ref, heads, gnw_ref[...], 0, dv_g, gla_scale)
    for p in range(n_hg // 2):
        lo = 2 * p * kd
        heads = (2 * p, 2 * p + 1)
        pair(g_hg[:, lo:lo + 2 * kd],
             _silu(p_ref[:, o_hq + lo:o_hq + lo + 2 * kd].astype(F32)).astype(BF16),
             k_hg[:, lo:lo + 2 * kd],
             [p_ref[:, o_hi + hd * dv_h:o_hi + (hd + 1) * dv_h] for hd in heads],
             [p_ref[:, o_hg + hd * dv_h:o_hg + (hd + 1) * dv_h] for hd in heads],
             sh_ref, heads, hnw_ref[...], gla_v, dv_h, 1.0)


def _mixer(p, w_gate_pad, b_gate, lb_logits, gla_norm_w, hg_norm_w, batch, seq):
    m = p.shape[0]
    c = MIX_CHUNK
    steps = seq // c
    wcs = jnp.asarray(_cumsum_matrix(c), BF16)
    masks = jnp.asarray(_level_masks(c), BF16)
    dv_g = gla_norm_w.shape[0]
    dv_h = hg_norm_w.shape[0]
    n_hg = lb_logits.shape[1] // HG_EXPAND
    width = GLA_HEADS * dv_g + n_hg * dv_h
    const2 = lambda b, t: (0, 0)
    return pl.pallas_call(
        functools.partial(_mixer_kernel, gla_scale=float(HG_EXPAND) ** -0.5),
        grid=(batch, steps),
        in_specs=[pl.BlockSpec((c, p.shape[1]), lambda b, t: (b * steps + t, 0)),
                  pl.BlockSpec(wcs.shape, const2),
                  pl.BlockSpec(masks.shape, lambda b, t: (0, 0, 0)),
                  pl.BlockSpec(w_gate_pad.shape, const2),
                  pl.BlockSpec((1, b_gate.shape[0]), const2),
                  pl.BlockSpec(lb_logits.shape, const2),
                  pl.BlockSpec((1, dv_g), const2),
                  pl.BlockSpec((1, dv_h), const2)],
        out_specs=pl.BlockSpec((c, width), lambda b, t: (b * steps + t, 0)),
        out_shape=jax.ShapeDtypeStruct((m, width), BF16),
        scratch_shapes=[pltpu.VMEM((GLA_HEADS, dv_g, HG_EXPAND), F32),
                        pltpu.VMEM((n_hg, dv_h, HG_EXPAND), F32)],
        compiler_params=pltpu.CompilerParams(
            dimension_semantics=("arbitrary", "arbitrary"), vmem_limit_bytes=VMEM_LIMIT),
        name="mixer",
    )(p, wcs, masks, w_gate_pad, b_gate.reshape(1, -1), lb_logits,
      gla_norm_w.reshape(1, -1), hg_norm_w.reshape(1, -1))


def _outproj_kernel(o_ref, w_ref, x_ref, gate_ref, x1_ref):
    x1_ref[...] = x_ref[...] + gate_ref[0] * _mm(o_ref[...], w_ref[...])


def _outproj(o, w_out, x2d, gate, seq, tm=512):
    m, d = x2d.shape
    tpb = seq // tm
    return pl.pallas_call(
        _outproj_kernel,
        grid=(m // tm,),
        in_specs=[pl.BlockSpec((tm, o.shape[1]), lambda i: (i, 0)),
                  pl.BlockSpec(w_out.shape, lambda i: (0, 0)),
                  pl.BlockSpec((tm, d), lambda i: (i, 0)),
                  pl.BlockSpec((1, 1, d), lambda i: (i // tpb, 0, 0))],
        out_specs=pl.BlockSpec((tm, d), lambda i: (i, 0)),
        out_shape=jax.ShapeDtypeStruct((m, d), F32),
        compiler_params=pltpu.CompilerParams(
            dimension_semantics=("arbitrary",), vmem_limit_bytes=VMEM_LIMIT),
        name="outproj",
    )(o, w_out, x2d, gate)


def _ffn_kernel(x_ref, nw_ref, sc_ref, sh_ref, gate_ref, fw_ref, wa_ref, wu_ref, wo_ref,
                out_ref, h_ref, acc_ref, *, nj):
    j = pl.program_id(1)

    @pl.when(j == 0)
    def _():
        h = _rms(x_ref[...], nw_ref[...]) * (1.0 + sc_ref[0]) + sh_ref[0]
        h_ref[...] = h.astype(BF16)
        acc_ref[...] = jnp.zeros_like(acc_ref)

    h = h_ref[...]
    act = _silu(_mm(h, wa_ref[...])) * _mm(h, wu_ref[...])
    acc_ref[...] += _mm(act.astype(BF16), wo_ref[...])

    @pl.when(j == nj - 1)
    def _():
        out_ref[...] = _rms(x_ref[...] + gate_ref[0] * acc_ref[...], fw_ref[...])


def _ffn(x1, norm_w, scale, shift, gate, final_w, w_in, w_out, seq, tm=512, tf=512):
    m, d = x1.shape
    hidden = w_out.shape[0]
    nj = hidden // tf
    tpb = seq // tm
    vec = pl.BlockSpec((1, d), lambda i, j: (0, 0))
    mod = pl.BlockSpec((1, 1, d), lambda i, j: (i // tpb, 0, 0))
    return pl.pallas_call(
        functools.partial(_ffn_kernel, nj=nj),
        grid=(m // tm, nj),
        in_specs=[pl.BlockSpec((tm, d), lambda i, j: (i, 0)),
                  vec, mod, mod, mod, vec,
                  pl.BlockSpec((d, tf), lambda i, j: (0, j)),
                  pl.BlockSpec((d, tf), lambda i, j: (0, j + nj)),
                  pl.BlockSpec((tf, d), lambda i, j: (j, 0))],
        out_specs=pl.BlockSpec((tm, d), lambda i, j: (i, 0)),
        out_shape=jax.ShapeDtypeStruct((m, d), F32),
        scratch_shapes=[pltpu.VMEM((tm, d), BF16), pltpu.VMEM((tm, d), F32)],
        compiler_params=pltpu.CompilerParams(
            dimension_semantics=("arbitrary", "arbitrary"), vmem_limit_bytes=VMEM_LIMIT),
        name="ffn",
    )(x1, norm_w.reshape(1, d), scale, shift, gate, final_w.reshape(1, d), w_in, w_in, w_out)


def kernel(x, c, w_ada, b_ada, norm_mix_w, w_in, w_gla_gate, b_gla_gate, gla_norm_w,
           hg_lower_bound_logits, hg_norm_w, w_out, norm_ffn_w, w_ffn_in, w_ffn_out,
           final_norm_w):
    batch, seq, d = x.shape
    assert w_ada.shape[0] == 1, "single-layer block"
    dv_g = gla_norm_w.shape[1]
    gla_k = GLA_HEADS * HG_EXPAND
    lr_end = 2 * gla_k + 2 * GLA_HEADS * dv_g + GLA_GATE_RANK
    x2d = x.reshape(batch * seq, d)

    mod = _ada(c, w_ada[0], b_ada[0])
    shift_m, scale_m, gate_m, shift_f, scale_f, gate_f = [
        mod[:, i * d:(i + 1) * d].reshape(batch, 1, d) for i in range(6)]

    pad = LANES - GLA_GATE_RANK
    wi = w_in[0]
    w_in_pad = jnp.concatenate(
        [wi[:, :lr_end], jnp.zeros((d, pad), wi.dtype), wi[:, lr_end:]], axis=1).astype(BF16)
    w_gate_pad = jnp.concatenate(
        [w_gla_gate[0], jnp.zeros((pad, gla_k), w_gla_gate.dtype)], axis=0).astype(BF16)

    p = _inproj(x2d, norm_mix_w[0], scale_m, shift_m, w_in_pad, seq)
    o = _mixer(p, w_gate_pad, b_gla_gate[0], hg_lower_bound_logits, gla_norm_w[0],
               hg_norm_w[0], batch, seq)
    x1 = _outproj(o, w_out[0].astype(BF16), x2d, gate_m, seq)
    out = _ffn(x1, norm_ffn_w[0], scale_f, shift_f, gate_f, final_norm_w,
               w_ffn_in[0].astype(BF16), w_ffn_out[0].astype(BF16), seq)
    return out.reshape(batch, seq, d)
```

```python
import functools

import numpy as np
import jax
import jax.numpy as jnp
from jax import lax
from jax.experimental import pallas as pl
from jax.experimental.pallas import tpu as pltpu

F32 = jnp.float32
BF16 = jnp.bfloat16

NORM_EPS = 1e-6
LOG2_E = 1.4426950408889634
GLA_HEADS = 4
GLA_GATE_RANK = 16
GLA_GATE_NORMALIZER = 16.0
HG_EXPAND = 128
LANES = 128
SUBLANES = 8
MXU_COLS = 256
MIX_CHUNK = 128
VMEM_LIMIT = 56 * 1024 * 1024


def _mm(a, b):
    return jnp.dot(a, b, preferred_element_type=F32)


def _mm_nt(a, b):
    return lax.dot_general(a, b, (((1,), (1,)), ((), ())), preferred_element_type=F32)


def _mm_tn(a, b):
    return lax.dot_general(a, b, (((0,), (0,)), ((), ())), preferred_element_type=F32)


def _silu(x):
    return x * jax.nn.sigmoid(x)


def _rms(x, w):
    var = jnp.mean(x * x, axis=-1, keepdims=True)
    return x * lax.rsqrt(var + NORM_EPS) * w


def _ada_kernel(cb_ref, w_ref, b_ref, o_ref):
    nb = cb_ref.shape[0]
    tn = w_ref.shape[1]
    acts = [_silu(cb_ref[b]) for b in range(nb)]
    for n in range(tn // LANES):
        w = w_ref[:, n * LANES:(n + 1) * LANES]
        for b in range(nb):
            o_ref[b:b + 1, n * LANES:(n + 1) * LANES] = (
                jnp.sum(w * acts[b], axis=0, keepdims=True)
                + b_ref[:, n * LANES:(n + 1) * LANES])


def _ada(c, w_ada, b_ada, tn=512):
    nb, d = c.shape
    n = w_ada.shape[1]
    cb = jnp.broadcast_to(c[:, :, None], (nb, d, LANES))
    return pl.pallas_call(
        _ada_kernel,
        grid=(n // tn,),
        in_specs=[pl.BlockSpec((nb, d, LANES), lambda j: (0, 0, 0)),
                  pl.BlockSpec((d, tn), lambda j: (0, j)),
                  pl.BlockSpec((1, tn), lambda j: (0, j))],
        out_specs=pl.BlockSpec((nb, tn), lambda j: (0, j)),
        out_shape=jax.ShapeDtypeStruct((nb, n), F32),
        compiler_params=pltpu.CompilerParams(
            dimension_semantics=("arbitrary",), vmem_limit_bytes=VMEM_LIMIT),
        name="ada",
    )(cb, w_ada, b_ada.reshape(1, n))


def _wprep_kernel(wt_ref, o_ref, *, lr_end, pad):
    n_in = wt_ref.shape[0]
    o_ref[:lr_end] = wt_ref[:lr_end].astype(BF16)
    o_ref[lr_end:lr_end + pad] = jnp.zeros((pad, o_ref.shape[1]), BF16)
    o_ref[lr_end + pad:] = wt_ref[lr_end:n_in].astype(BF16)


def _wprep(wt, lr_end, pad, tc=256):
    n_in, d = wt.shape
    return pl.pallas_call(
        functools.partial(_wprep_kernel, lr_end=lr_end, pad=pad),
        grid=(d // tc,),
        in_specs=[pl.BlockSpec((n_in, tc), lambda i: (0, i))],
        out_specs=pl.BlockSpec((n_in + pad, tc), lambda i: (0, i)),
        out_shape=jax.ShapeDtypeStruct((n_in + pad, d), BF16),
        compiler_params=pltpu.CompilerParams(
            dimension_semantics=("arbitrary",), vmem_limit_bytes=VMEM_LIMIT),
        name="wprep",
    )(wt)


def _norm_modulate_rows(dst_ref, x_ref, row0, rows, nw, scale1, shift):
    wmod = nw * scale1
    slab = 2 * SUBLANES
    for s in range(rows // slab):
        r = pl.ds(row0 + s * slab, slab)
        x = x_ref[r, :]
        var = jnp.mean(x * x, axis=-1, keepdims=True)
        dst_ref[r, :] = (x * lax.rsqrt(var + NORM_EPS) * wmod + shift).astype(dst_ref.dtype)


def _inproj_kernel(x0_ref, xn_ref, nw_ref, sc0_ref, sh0_ref, scn_ref, shn_ref, wt_ref, o_ref,
                   h_ref, *, n_tiles):
    i, j = pl.program_id(0), pl.program_id(1)
    tm = x0_ref.shape[0]

    @pl.when((i == 0) & (j == 0))
    def _():
        _norm_modulate_rows(h_ref.at[0], x0_ref, 0, tm, nw_ref[...], 1.0 + sc0_ref[0], sh0_ref[0])

    o_ref[...] = _mm_nt(h_ref[i % 2], wt_ref[...]).astype(o_ref.dtype)

    share = -(-tm // (n_tiles * 32)) * 32
    row0 = pl.multiple_of(jnp.minimum(j * share, tm - share), 32)
    _norm_modulate_rows(h_ref.at[(i + 1) % 2], xn_ref, row0, share, nw_ref[...],
                        1.0 + scn_ref[0], shn_ref[0])


def _inproj(x2d, norm_w, scale, shift, wt, seq, tm=512, n_tiles=3):
    m, d = x2d.shape
    n = wt.shape[0]
    tn = n // n_tiles
    tpb = seq // tm
    nm = m // tm
    nxt = lambda i: jnp.minimum(i + 1, nm - 1)
    return pl.pallas_call(
        functools.partial(_inproj_kernel, n_tiles=n_tiles),
        grid=(nm, n_tiles),
        in_specs=[pl.BlockSpec((tm, d), lambda i, j: (0, 0)),
                  pl.BlockSpec((tm, d), lambda i, j: (nxt(i), 0)),
                  pl.BlockSpec((1, d), lambda i, j: (0, 0)),
                  pl.BlockSpec((1, 1, d), lambda i, j: (0, 0, 0)),
                  pl.BlockSpec((1, 1, d), lambda i, j: (0, 0, 0)),
                  pl.BlockSpec((1, 1, d), lambda i, j: (nxt(i) // tpb, 0, 0)),
                  pl.BlockSpec((1, 1, d), lambda i, j: (nxt(i) // tpb, 0, 0)),
                  pl.BlockSpec((tn, d), lambda i, j: (j, 0))],
        out_specs=pl.BlockSpec((tm, tn), lambda i, j: (i, j)),
        out_shape=jax.ShapeDtypeStruct((m, n), BF16),
        scratch_shapes=[pltpu.VMEM((2, tm, d), BF16)],
        compiler_params=pltpu.CompilerParams(
            dimension_semantics=("arbitrary", "arbitrary"), vmem_limit_bytes=VMEM_LIMIT),
        name="inproj",
    )(x2d, x2d, norm_w.reshape(1, d), scale, shift, scale, shift, wt)


def _small_levels(c):
    return [h for h in (SUBLANES // 2, SUBLANES // 4, SUBLANES // 8) if 1 <= h < c]


def _big_levels(c):
    out, h = [], c // 2
    while h >= SUBLANES:
        out.append(h)
        h //= 2
    return out


def _cumsum_matrix(c):
    t = np.arange(c)[:, None]
    s = np.arange(c)[None, :]
    blocks = [s <= t]
    for h in _small_levels(c):
        mid = (t // (2 * h)) * (2 * h) + h - 1
        second = (t % (2 * h)) >= h
        blocks.append(np.where(second, (s > mid) & (s <= t), (s > t) & (s <= mid)))
    return np.concatenate(blocks, axis=0).astype(np.float32)


def _level_masks(c):
    i = np.arange(c)[:, None]
    j = np.arange(c)[None, :]
    masks = [i == j]
    for h in _big_levels(c) + _small_levels(c):
        same = (i // (2 * h)) == (j // (2 * h))
        masks.append(same & ((i % (2 * h)) >= h) & ((j % (2 * h)) < h))
    m = np.stack(masks).astype(np.float32)
    return np.concatenate([m, m], axis=2)


def _mixer_kernel(p_ref, wcs_ref, mask_ref, wgate_ref, bgate_ref, lbl_ref,
                  gnw_ref, hnw_ref, o_ref, sg_ref, sh_ref, *, gla_scale):
    c = p_ref.shape[0]
    kd = HG_EXPAND
    n_gla, dv_g = sg_ref.shape[0], sg_ref.shape[1]
    n_hg, dv_h = sh_ref.shape[0], sh_ref.shape[1]
    gla_k, gla_v, hg_w = n_gla * kd, n_gla * dv_g, n_hg * kd
    o_gq, o_gk, o_gv, o_gg = 0, gla_k, 2 * gla_k, 2 * gla_k + gla_v
    o_lr = o_gg + gla_v
    o_hq = o_lr + LANES
    o_hf = o_hq + hg_w
    o_hi = o_hf + hg_w
    o_hg = o_hi + hg_w
    big = _big_levels(c)
    small = _small_levels(c)

    @pl.when(pl.program_id(1) == 0)
    def _():
        sg_ref[...] = jnp.zeros_like(sg_ref)
        sh_ref[...] = jnp.zeros_like(sh_ref)

    z = _mm(p_ref[:, o_lr:o_lr + LANES], wgate_ref[...]) + bgate_ref[...]
    g_gla = (jnp.minimum(z, 0.0) - jnp.log1p(jnp.exp(-jnp.abs(z)))) * (LOG2_E / GLA_GATE_NORMALIZER)
    lbl = lbl_ref[...]
    lbe = jnp.exp(lbl - jnp.max(lbl, axis=0, keepdims=True))
    lb = lbe[0:1, :] / jnp.sum(lbe, axis=0, keepdims=True)
    f = lb + (1.0 - lb) * jax.nn.sigmoid(p_ref[:, o_hf:o_hf + hg_w].astype(F32))
    g_hg = jnp.log2(f)
    k_hg = (1.0 - f).astype(BF16)

    wcs = wcs_ref[...]
    zero = jnp.zeros((c, kd), BF16)

    def blockdiag(kk):
        return jnp.concatenate(
            [jnp.concatenate([kk[:, :kd], zero], axis=1),
             jnp.concatenate([zero, kk[:, kd:]], axis=1)], axis=0)

    def pair(g2, qb, kb, vs, gates, st_ref, heads, nw, out_off, dv, q_scale):
        g_hi = g2.astype(BF16)
        g_lo = (g2 - g_hi.astype(F32)).astype(BF16)
        cs = _mm(wcs, g_hi)
        b = cs[0:c] + _mm(wcs[0:c], g_lo)
        exps = []
        for h in big:
            parts = []
            for s in range(0, c, 2 * h):
                mid = jnp.broadcast_to(b[s + h - 1:s + h, :], (h, 2 * kd))
                parts += [mid - b[s:s + h], b[s + h:s + 2 * h] - mid]
            exps.append(jnp.exp2(jnp.concatenate(parts, axis=0)))
        for i in range(len(small)):
            exps.append(jnp.exp2(cs[(1 + i) * c:(2 + i) * c]))
        e_b = jnp.exp2(b)
        e_r = jnp.exp2(jnp.broadcast_to(b[c - 1:c, :], (c, 2 * kd)) - b)

        s = jnp.where(mask_ref[0] > 0, _mm_nt(qb, blockdiag(kb)).astype(BF16), 0.0)
        for lv, e in enumerate(exps):
            eb = e.astype(BF16)
            s = jnp.where(mask_ref[1 + lv] > 0,
                          _mm_nt(qb * eb, blockdiag(kb * eb)).astype(BF16), s)
        qe = qb * e_b.astype(BF16)
        ke = kb * e_r.astype(BF16)
        for i, hd in enumerate(heads):
            sl = slice(i * kd, (i + 1) * kd)
            st = st_ref[hd]
            o = _mm(s[:, i * c:(i + 1) * c], vs[i]) + _mm_nt(qe[:, sl], st.astype(BF16))
            st_ref[hd] = st * e_b[c - 1:c, sl] + _mm_tn(vs[i], ke[:, sl])
            var = jnp.mean(o * o, axis=-1, keepdims=True)
            r = q_scale * lax.rsqrt((q_scale * q_scale) * var + NORM_EPS)
            on = o * r * nw * _silu(gates[i].astype(F32))
            o_ref[:, out_off + hd * dv:out_off + (hd + 1) * dv] = on.astype(o_ref.dtype)

    for p in range(n_gla // 2):
        lo = 2 * p * kd
        heads = (2 * p, 2 * p + 1)
        pair(g_gla[:, lo:lo + 2 * kd],
             p_ref[:, o_gq + lo:o_gq + lo + 2 * kd],
             p_ref[:, o_gk + lo:o_gk + lo + 2 * kd],
             [p_ref[:, o_gv + hd * dv_g:o_gv + (hd + 1) * dv_g] for hd in heads],
             [p_ref[:, o_gg + hd * dv_g:o_gg + (hd + 1) * dv_g] for hd in heads],
             sg_ref, heads, gnw_ref[...], 0, dv_g, gla_scale)
    for p in range(n_hg // 2):
        lo = 2 * p * kd
        heads = (2 * p, 2 * p + 1)
        pair(g_hg[:, lo:lo + 2 * kd],
             _silu(p_ref[:, o_hq + lo:o_hq + lo + 2 * kd].astype(F32)).astype(BF16),
             k_hg[:, lo:lo + 2 * kd],
             [p_ref[:, o_hi + hd * dv_h:o_hi + (hd + 1) * dv_h] for hd in heads],
             [p_ref[:, o_hg + hd * dv_h:o_hg + (hd + 1) * dv_h] for hd in heads],
             sh_ref, heads, hnw_ref[...], gla_v, dv_h, 1.0)


def _mixer(p, w_gate_pad, b_gate, lb_logits, gla_norm_w, hg_norm_w, batch, seq):
    m = p.shape[0]
    c = MIX_CHUNK
    steps = seq // c
    wcs = jnp.asarray(_cumsum_matrix(c), BF16)
    masks = jnp.asarray(_level_masks(c), BF16)
    dv_g = gla_norm_w.shape[0]
    dv_h = hg_norm_w.shape[0]
    n_hg = lb_logits.shape[1] // HG_EXPAND
    width = GLA_HEADS * dv_g + n_hg * dv_h
    const2 = lambda b, t: (0, 0)
    return pl.pallas_call(
        functools.partial(_mixer_kernel, gla_scale=float(HG_EXPAND) ** -0.5),
        grid=(batch, steps),
        in_specs=[pl.BlockSpec((c, p.shape[1]), lambda b, t: (b * steps + t, 0)),
                  pl.BlockSpec(wcs.shape, const2),
                  pl.BlockSpec(masks.shape, lambda b, t: (0, 0, 0)),
                  pl.BlockSpec(w_gate_pad.shape, const2),
                  pl.BlockSpec((1, b_gate.shape[0]), const2),
                  pl.BlockSpec(lb_logits.shape, const2),
                  pl.BlockSpec((1, dv_g), const2),
                  pl.BlockSpec((1, dv_h), const2)],
        out_specs=pl.BlockSpec((c, width), lambda b, t: (b * steps + t, 0)),
        out_shape=jax.ShapeDtypeStruct((m, width), BF16),
        scratch_shapes=[pltpu.VMEM((GLA_HEADS, dv_g, HG_EXPAND), F32),
                        pltpu.VMEM((n_hg, dv_h, HG_EXPAND), F32)],
        compiler_params=pltpu.CompilerParams(
            dimension_semantics=("arbitrary", "arbitrary"), vmem_limit_bytes=VMEM_LIMIT),
        name="mixer",
    )(p, wcs, masks, w_gate_pad, b_gate.reshape(1, -1), lb_logits,
      gla_norm_w.reshape(1, -1), hg_norm_w.reshape(1, -1))


def _outproj_kernel(o_ref, w_ref, x_ref, gate_ref, nw_ref, sc_ref, sh_ref, x1_ref, h_ref):
    x1_ref[...] = x_ref[...] + gate_ref[0] * _mm(o_ref[...], w_ref[...])
    _norm_modulate_rows(h_ref, x1_ref, 0, x1_ref.shape[0], nw_ref[...], 1.0 + sc_ref[0], sh_ref[0])


def _outproj(o, w_out, x2d, gate, norm_w, scale, shift, seq, tm=512):
    m, d = x2d.shape
    tpb = seq // tm
    row = pl.BlockSpec((tm, d), lambda i: (i, 0))
    mod = pl.BlockSpec((1, 1, d), lambda i: (i // tpb, 0, 0))
    return pl.pallas_call(
        _outproj_kernel,
        grid=(m // tm,),
        in_specs=[pl.BlockSpec((tm, o.shape[1]), lambda i: (i, 0)),
                  pl.BlockSpec(w_out.shape, lambda i: (0, 0)),
                  row, mod, pl.BlockSpec((1, d), lambda i: (0, 0)), mod, mod],
        out_specs=[row, row],
        out_shape=[jax.ShapeDtypeStruct((m, d), F32), jax.ShapeDtypeStruct((m, d), BF16)],
        compiler_params=pltpu.CompilerParams(
            dimension_semantics=("arbitrary",), vmem_limit_bytes=VMEM_LIMIT),
        name="outproj",
    )(o, w_out, x2d, gate, norm_w.reshape(1, d), scale, shift)


def _ffn_kernel(h_ref, x_ref, gate_ref, fw_ref, wa_ref, wu_ref, wo_ref, out_ref, acc_ref, *, nj):
    j = pl.program_id(1)

    @pl.when(j == 0)
    def _():
        acc_ref[...] = jnp.zeros_like(acc_ref)

    h = h_ref[...]
    act = _silu(_mm(h, wa_ref[...])) * _mm(h, wu_ref[...])
    acc_ref[...] += _mm(act.astype(BF16), wo_ref[...])

    @pl.when(j == nj - 1)
    def _():
        gate, fw = gate_ref[0], fw_ref[...]
        slab = 2 * SUBLANES
        for s in range(x_ref.shape[0] // slab):
            r = pl.ds(s * slab, slab)
            x = x_ref[r, :] + gate * acc_ref[r, :]
            var = jnp.mean(x * x, axis=-1, keepdims=True)
            out_ref[r, :] = x * lax.rsqrt(var + NORM_EPS) * fw


def _ffn(h, x1, gate, final_w, w_in, w_out, seq, tm=512, tf=512):
    m, d = x1.shape
    hidden = w_out.shape[0]
    nj = hidden // tf
    tpb = seq // tm
    row = pl.BlockSpec((tm, d), lambda i, j: (i, 0))
    return pl.pallas_call(
        functools.partial(_ffn_kernel, nj=nj),
        grid=(m // tm, nj),
        in_specs=[row, row,
                  pl.BlockSpec((1, 1, d), lambda i, j: (i // tpb, 0, 0)),
                  pl.BlockSpec((1, d), lambda i, j: (0, 0)),
                  pl.BlockSpec((d, tf), lambda i, j: (0, j)),
                  pl.BlockSpec((d, tf), lambda i, j: (0, j + nj)),
                  pl.BlockSpec((tf, d), lambda i, j: (j, 0))],
        out_specs=row,
        out_shape=jax.ShapeDtypeStruct((m, d), F32),
        scratch_shapes=[pltpu.VMEM((tm, d), F32)],
        compiler_params=pltpu.CompilerParams(
            dimension_semantics=("arbitrary", "arbitrary"), vmem_limit_bytes=VMEM_LIMIT),
        name="ffn",
    )(h, x1, gate, final_w.reshape(1, d), w_in, w_in, w_out)


def kernel(x, c, w_ada, b_ada, norm_mix_w, w_in, w_gla_gate, b_gla_gate, gla_norm_w,
           hg_lower_bound_logits, hg_norm_w, w_out, norm_ffn_w, w_ffn_in, w_ffn_out,
           final_norm_w):
    batch, seq, d = x.shape
    assert w_ada.shape[0] == 1, "single-layer block"
    dv_g = gla_norm_w.shape[1]
    gla_k = GLA_HEADS * HG_EXPAND
    lr_end = 2 * gla_k + 2 * GLA_HEADS * dv_g + GLA_GATE_RANK
    x2d = x.reshape(batch * seq, d)

    mod = _ada(c, w_ada[0], b_ada[0])
    shift_m, scale_m, gate_m, shift_f, scale_f, gate_f = [
        mod[:, i * d:(i + 1) * d].reshape(batch, 1, d) for i in range(6)]

    pad = LANES - GLA_GATE_RANK
    w_in_pad = _wprep(jnp.transpose(w_in[0]), lr_end, pad)
    w_gate_pad = jnp.concatenate(
        [w_gla_gate[0], jnp.zeros((pad, gla_k), w_gla_gate.dtype)], axis=0).astype(BF16)

    p = _inproj(x2d, norm_mix_w[0], scale_m, shift_m, w_in_pad, seq)
    o = _mixer(p, w_gate_pad, b_gla_gate[0], hg_lower_bound_logits, gla_norm_w[0],
               hg_norm_w[0], batch, seq)
    x1, h2 = _outproj(o, w_out[0].astype(BF16), x2d, gate_m, norm_ffn_w[0], scale_f, shift_f, seq)
    out = _ffn(h2, x1, gate_f, final_norm_w,
               w_ffn_in[0].astype(BF16), w_ffn_out[0].astype(BF16), seq)
    return out.reshape(batch, seq, d)
```

```python
import functools

import numpy as np
import jax
import jax.numpy as jnp
from jax import lax
from jax.experimental import pallas as pl
from jax.experimental.pallas import tpu as pltpu

F32 = jnp.float32
BF16 = jnp.bfloat16

NORM_EPS = 1e-6
LOG2_E = 1.4426950408889634
GLA_HEADS = 4
GLA_GATE_RANK = 16
GLA_GATE_NORMALIZER = 16.0
HG_EXPAND = 128
LANES = 128
SUBLANES = 8
MXU_COLS = 256
MIX_CHUNK = 128
VMEM_LIMIT = 56 * 1024 * 1024


def _mm(a, b):
    return jnp.dot(a, b, preferred_element_type=F32)


def _mm_nt(a, b):
    return lax.dot_general(a, b, (((1,), (1,)), ((), ())), preferred_element_type=F32)


def _mm_tn(a, b):
    return lax.dot_general(a, b, (((0,), (0,)), ((), ())), preferred_element_type=F32)


def _silu(x):
    return x * jax.nn.sigmoid(x)


def _rms(x, w):
    var = jnp.mean(x * x, axis=-1, keepdims=True)
    return x * lax.rsqrt(var + NORM_EPS) * w


def _ada_kernel(cb_ref, w_ref, b_ref, o_ref, act_ref):
    nb = cb_ref.shape[0]
    tn = w_ref.shape[1]

    @pl.when(pl.program_id(0) == 0)
    def _():
        act_ref[...] = _silu(cb_ref[...])

    for n in range(tn // LANES):
        w = w_ref[:, n * LANES:(n + 1) * LANES]
        for b in range(nb):
            o_ref[b:b + 1, n * LANES:(n + 1) * LANES] = (
                jnp.sum(w * act_ref[b], axis=0, keepdims=True)
                + b_ref[:, n * LANES:(n + 1) * LANES])


def _ada(c, w_ada, b_ada, tn=1024):
    nb, d = c.shape
    n = w_ada.shape[1]
    cb = jnp.broadcast_to(c[:, :, None], (nb, d, LANES))
    return pl.pallas_call(
        _ada_kernel,
        grid=(n // tn,),
        in_specs=[pl.BlockSpec((nb, d, LANES), lambda j: (0, 0, 0)),
                  pl.BlockSpec((d, tn), lambda j: (0, j)),
                  pl.BlockSpec((1, tn), lambda j: (0, j))],
        out_specs=pl.BlockSpec((nb, tn), lambda j: (0, j)),
        out_shape=jax.ShapeDtypeStruct((nb, n), F32),
        scratch_shapes=[pltpu.VMEM((nb, d, LANES), F32)],
        compiler_params=pltpu.CompilerParams(
            dimension_semantics=("arbitrary",), vmem_limit_bytes=VMEM_LIMIT),
        name="ada",
    )(cb, w_ada, b_ada.reshape(1, n))


def _wprep_kernel(wt_ref, o_ref, *, lr_end, pad):
    n_in = wt_ref.shape[0]
    o_ref[:lr_end] = wt_ref[:lr_end].astype(BF16)
    o_ref[lr_end:lr_end + pad] = jnp.zeros((pad, o_ref.shape[1]), BF16)
    o_ref[lr_end + pad:] = wt_ref[lr_end:n_in].astype(BF16)


def _wprep(wt, lr_end, pad, tc=256):
    n_in, d = wt.shape
    return pl.pallas_call(
        functools.partial(_wprep_kernel, lr_end=lr_end, pad=pad),
        grid=(d // tc,),
        in_specs=[pl.BlockSpec((n_in, tc), lambda i: (0, i))],
        out_specs=pl.BlockSpec((n_in + pad, tc), lambda i: (0, i)),
        out_shape=jax.ShapeDtypeStruct((n_in + pad, d), BF16),
        compiler_params=pltpu.CompilerParams(
            dimension_semantics=("arbitrary",), vmem_limit_bytes=VMEM_LIMIT),
        name="wprep",
    )(wt)


def _norm_modulate_rows(dst_ref, x_ref, row0, rows, nw, scale1, shift):
    wmod = nw * scale1
    slab = 2 * SUBLANES
    for s in range(rows // slab):
        r = pl.ds(row0 + s * slab, slab)
        x = x_ref[r, :]
        var = jnp.mean(x * x, axis=-1, keepdims=True)
        dst_ref[r, :] = (x * lax.rsqrt(var + NORM_EPS) * wmod + shift).astype(dst_ref.dtype)


def _inproj_kernel(x0_ref, xn_ref, nw_ref, sc0_ref, sh0_ref, scn_ref, shn_ref, wt_ref, o_ref,
                   h_ref, *, n_tiles):
    i, j = pl.program_id(0), pl.program_id(1)
    tm = x0_ref.shape[0]

    @pl.when((i == 0) & (j == 0))
    def _():
        _norm_modulate_rows(h_ref.at[0], x0_ref, 0, tm, nw_ref[...], 1.0 + sc0_ref[0], sh0_ref[0])

    o_ref[...] = _mm_nt(h_ref[i % 2], wt_ref[...]).astype(o_ref.dtype)

    share = -(-tm // (n_tiles * 32)) * 32
    row0 = pl.multiple_of(jnp.minimum(j * share, tm - share), 32)
    _norm_modulate_rows(h_ref.at[(i + 1) % 2], xn_ref, row0, share, nw_ref[...],
                        1.0 + scn_ref[0], shn_ref[0])


def _inproj(x2d, norm_w, scale, shift, wt, seq, tm=512, n_tiles=3):
    m, d = x2d.shape
    n = wt.shape[0]
    tn = n // n_tiles
    tpb = seq // tm
    nm = m // tm
    nxt = lambda i: jnp.minimum(i + 1, nm - 1)
    return pl.pallas_call(
        functools.partial(_inproj_kernel, n_tiles=n_tiles),
        grid=(nm, n_tiles),
        in_specs=[pl.BlockSpec((tm, d), lambda i, j: (0, 0)),
                  pl.BlockSpec((tm, d), lambda i, j: (nxt(i), 0)),
                  pl.BlockSpec((1, d), lambda i, j: (0, 0)),
                  pl.BlockSpec((1, 1, d), lambda i, j: (0, 0, 0)),
                  pl.BlockSpec((1, 1, d), lambda i, j: (0, 0, 0)),
                  pl.BlockSpec((1, 1, d), lambda i, j: (nxt(i) // tpb, 0, 0)),
                  pl.BlockSpec((1, 1, d), lambda i, j: (nxt(i) // tpb, 0, 0)),
                  pl.BlockSpec((tn, d), lambda i, j: (j, 0))],
        out_specs=pl.BlockSpec((tm, tn), lambda i, j: (i, j)),
        out_shape=jax.ShapeDtypeStruct((m, n), BF16),
        scratch_shapes=[pltpu.VMEM((2, tm, d), BF16)],
        compiler_params=pltpu.CompilerParams(
            dimension_semantics=("arbitrary", "arbitrary"), vmem_limit_bytes=VMEM_LIMIT),
        name="inproj",
    )(x2d, x2d, norm_w.reshape(1, d), scale, shift, scale, shift, wt)


def _small_levels(c):
    return [h for h in (SUBLANES // 2, SUBLANES // 4, SUBLANES // 8) if 1 <= h < c]


def _big_levels(c):
    out, h = [], c // 2
    while h >= SUBLANES:
        out.append(h)
        h //= 2
    return out


def _cumsum_matrix(c):
    t = np.arange(c)[:, None]
    s = np.arange(c)[None, :]
    blocks = [s <= t]
    for h in _small_levels(c):
        mid = (t // (2 * h)) * (2 * h) + h - 1
        second = (t % (2 * h)) >= h
        blocks.append(np.where(second, (s > mid) & (s <= t), (s > t) & (s <= mid)))
    return np.concatenate(blocks, axis=0).astype(np.float32)


def _level_masks(c):
    i = np.arange(c)[:, None]
    j = np.arange(c)[None, :]
    masks = [i == j]
    for h in _big_levels(c) + _small_levels(c):
        same = (i // (2 * h)) == (j // (2 * h))
        masks.append(same & ((i % (2 * h)) >= h) & ((j % (2 * h)) < h))
    m = np.stack(masks).astype(np.float32)
    return np.concatenate([m, m], axis=2)


def _mixer_kernel(p_ref, wcs_ref, mask_ref, wgate_ref, bgate_ref, lbl_ref,
                  gnw_ref, hnw_ref, o_ref, sg_ref, sh_ref, *, gla_scale):
    c = p_ref.shape[0]
    kd = HG_EXPAND
    n_gla, dv_g = sg_ref.shape[0], sg_ref.shape[1]
    n_hg, dv_h = sh_ref.shape[0], sh_ref.shape[1]
    gla_k, gla_v, hg_w = n_gla * kd, n_gla * dv_g, n_hg * kd
    o_gq, o_gk, o_gv, o_gg = 0, gla_k, 2 * gla_k, 2 * gla_k + gla_v
    o_lr = o_gg + gla_v
    o_hq = o_lr + LANES
    o_hf = o_hq + hg_w
    o_hi = o_hf + hg_w
    o_hg = o_hi + hg_w
    big = _big_levels(c)
    small = _small_levels(c)

    @pl.when(pl.program_id(1) == 0)
    def _():
        sg_ref[...] = jnp.zeros_like(sg_ref)
        sh_ref[...] = jnp.zeros_like(sh_ref)

    z = _mm(p_ref[:, o_lr:o_lr + LANES], wgate_ref[...]) + bgate_ref[...]
    g_gla = (jnp.minimum(z, 0.0) - jnp.log1p(jnp.exp(-jnp.abs(z)))) * (LOG2_E / GLA_GATE_NORMALIZER)
    lbl = lbl_ref[...]
    lbe = jnp.exp(lbl - jnp.max(lbl, axis=0, keepdims=True))
    lb = lbe[0:1, :] / jnp.sum(lbe, axis=0, keepdims=True)
    f = lb + (1.0 - lb) * jax.nn.sigmoid(p_ref[:, o_hf:o_hf + hg_w].astype(F32))
    g_hg = jnp.log2(f)
    k_hg = (1.0 - f).astype(BF16)

    wcs = wcs_ref[...]
    zero = jnp.zeros((kd, c), BF16)

    def blockdiag_t(kk):
        return jnp.concatenate(
            [jnp.concatenate([kk[:, :kd].T, zero], axis=1),
             jnp.concatenate([zero, kk[:, kd:].T], axis=1)], axis=0)

    def decays(g2):
        g_hi = g2.astype(BF16)
        g_lo = (g2 - g_hi.astype(F32)).astype(BF16)
        cs = _mm(wcs, g_hi)
        b = cs[0:c] + _mm(wcs[0:c], g_lo)
        exps = []
        for h in big:
            parts = []
            for s in range(0, c, 2 * h):
                mid = jnp.broadcast_to(b[s + h - 1:s + h, :], (h, 2 * kd))
                parts += [mid - b[s:s + h], b[s + h:s + 2 * h] - mid]
            exps.append(jnp.exp2(jnp.concatenate(parts, axis=0)).astype(BF16))
        for i in range(len(small)):
            exps.append(jnp.exp2(cs[(1 + i) * c:(2 + i) * c]).astype(BF16))
        e_b = jnp.exp2(b)
        e_r = jnp.exp2(jnp.broadcast_to(b[c - 1:c, :], (c, 2 * kd)) - b)
        return exps, e_b, e_r

    def scores(qb, kb, exps):
        s = jnp.where(mask_ref[0] > 0, _mm(qb, blockdiag_t(kb)).astype(BF16), 0.0)
        for lv, eb in enumerate(exps):
            s = jnp.where(mask_ref[1 + lv] > 0,
                          _mm(qb * eb, blockdiag_t(kb * eb)).astype(BF16), s)
        return s

    def outputs(s, qb, kb, e_b, e_r, vs, gates, st_ref, heads, nw, out_off, dv, q_scale):
        qe = qb * e_b.astype(BF16)
        ke = kb * e_r.astype(BF16)
        for i, hd in enumerate(heads):
            sl = slice(i * kd, (i + 1) * kd)
            st = st_ref[hd]
            o = _mm(s[:, i * c:(i + 1) * c], vs[i]) + _mm_nt(qe[:, sl], st.astype(BF16))
            st_ref[hd] = st * e_b[c - 1:c, sl] + _mm_tn(vs[i], ke[:, sl])
            var = jnp.mean(o * o, axis=-1, keepdims=True)
            r = q_scale * lax.rsqrt((q_scale * q_scale) * var + NORM_EPS)
            on = o * r * nw * _silu(gates[i].astype(F32))
            o_ref[:, out_off + hd * dv:out_off + (hd + 1) * dv] = on.astype(o_ref.dtype)

    pairs = []
    for p in range(n_gla // 2):
        lo = 2 * p * kd
        heads = (2 * p, 2 * p + 1)
        pairs.append(dict(
            g2=g_gla[:, lo:lo + 2 * kd],
            qb=p_ref[:, o_gq + lo:o_gq + lo + 2 * kd],
            kb=p_ref[:, o_gk + lo:o_gk + lo + 2 * kd],
            vs=[p_ref[:, o_gv + hd * dv_g:o_gv + (hd + 1) * dv_g] for hd in heads],
            gates=[p_ref[:, o_gg + hd * dv_g:o_gg + (hd + 1) * dv_g] for hd in heads],
            tail=(sg_ref, heads, gnw_ref[...], 0, dv_g, gla_scale)))
    for p in range(n_hg // 2):
        lo = 2 * p * kd
        heads = (2 * p, 2 * p + 1)
        pairs.append(dict(
            g2=g_hg[:, lo:lo + 2 * kd],
            qb=_silu(p_ref[:, o_hq + lo:o_hq + lo + 2 * kd].astype(F32)).astype(BF16),
            kb=k_hg[:, lo:lo + 2 * kd],
            vs=[p_ref[:, o_hi + hd * dv_h:o_hi + (hd + 1) * dv_h] for hd in heads],
            gates=[p_ref[:, o_hg + hd * dv_h:o_hg + (hd + 1) * dv_h] for hd in heads],
            tail=(sh_ref, heads, hnw_ref[...], gla_v, dv_h, 1.0)))
    dec = [decays(pr["g2"]) for pr in pairs]
    sc = [scores(pr["qb"], pr["kb"], d[0]) for pr, d in zip(pairs, dec)]
    for pr, d, s in zip(pairs, dec, sc):
        outputs(s, pr["qb"], pr["kb"], d[1], d[2], pr["vs"], pr["gates"], *pr["tail"])


def _mixer(p, w_gate_pad, b_gate, lb_logits, gla_norm_w, hg_norm_w, batch, seq):
    m = p.shape[0]
    c = MIX_CHUNK
    steps = seq // c
    wcs = jnp.asarray(_cumsum_matrix(c), BF16)
    masks = jnp.asarray(_level_masks(c), BF16)
    dv_g = gla_norm_w.shape[0]
    dv_h = hg_norm_w.shape[0]
    n_hg = lb_logits.shape[1] // HG_EXPAND
    width = GLA_HEADS * dv_g + n_hg * dv_h
    const2 = lambda b, t: (0, 0)
    return pl.pallas_call(
        functools.partial(_mixer_kernel, gla_scale=float(HG_EXPAND) ** -0.5),
        grid=(batch, steps),
        in_specs=[pl.BlockSpec((c, p.shape[1]), lambda b, t: (b * steps + t, 0)),
                  pl.BlockSpec(wcs.shape, const2),
                  pl.BlockSpec(masks.shape, lambda b, t: (0, 0, 0)),
                  pl.BlockSpec(w_gate_pad.shape, const2),
                  pl.BlockSpec((1, b_gate.shape[0]), const2),
                  pl.BlockSpec(lb_logits.shape, const2),
                  pl.BlockSpec((1, dv_g), const2),
                  pl.BlockSpec((1, dv_h), const2)],
        out_specs=pl.BlockSpec((c, width), lambda b, t: (b * steps + t, 0)),
        out_shape=jax.ShapeDtypeStruct((m, width), BF16),
        scratch_shapes=[pltpu.VMEM((GLA_HEADS, dv_g, HG_EXPAND), F32),
                        pltpu.VMEM((n_hg, dv_h, HG_EXPAND), F32)],
        compiler_params=pltpu.CompilerParams(
            dimension_semantics=("arbitrary", "arbitrary"), vmem_limit_bytes=VMEM_LIMIT),
        name="mixer",
    )(p, wcs, masks, w_gate_pad, b_gate.reshape(1, -1), lb_logits,
      gla_norm_w.reshape(1, -1), hg_norm_w.reshape(1, -1))


def _outproj_kernel(o_ref, w_ref, x_ref, gate_ref, nw_ref, sc_ref, sh_ref, x1_ref, h_ref):
    x1_ref[...] = x_ref[...] + gate_ref[0] * _mm(o_ref[...], w_ref[...])
    _norm_modulate_rows(h_ref, x1_ref, 0, x1_ref.shape[0], nw_ref[...], 1.0 + sc_ref[0], sh_ref[0])


def _outproj(o, w_out, x2d, gate, norm_w, scale, shift, seq, tm=512):
    m, d = x2d.shape
    tpb = seq // tm
    row = pl.BlockSpec((tm, d), lambda i: (i, 0))
    mod = pl.BlockSpec((1, 1, d), lambda i: (i // tpb, 0, 0))
    return pl.pallas_call(
        _outproj_kernel,
        grid=(m // tm,),
        in_specs=[pl.BlockSpec((tm, o.shape[1]), lambda i: (i, 0)),
                  pl.BlockSpec(w_out.shape, lambda i: (0, 0)),
                  row, mod, pl.BlockSpec((1, d), lambda i: (0, 0)), mod, mod],
        out_specs=[row, row],
        out_shape=[jax.ShapeDtypeStruct((m, d), F32), jax.ShapeDtypeStruct((m, d), BF16)],
        compiler_params=pltpu.CompilerParams(
            dimension_semantics=("arbitrary",), vmem_limit_bytes=VMEM_LIMIT),
        name="outproj",
    )(o, w_out, x2d, gate, norm_w.reshape(1, d), scale, shift)


def _ffn_kernel(h_ref, x_ref, gate_ref, fw_ref, wa_ref, wu_ref, wo_ref, out_ref, acc_ref, *, nj):
    j = pl.program_id(1)

    @pl.when(j == 0)
    def _():
        acc_ref[...] = jnp.zeros_like(acc_ref)

    h = h_ref[...]
    act = _silu(_mm(h, wa_ref[...])) * _mm(h, wu_ref[...])
    acc_ref[...] += _mm(act.astype(BF16), wo_ref[...])

    @pl.when(j == nj - 1)
    def _():
        gate, fw = gate_ref[0], fw_ref[...]
        slab = 2 * SUBLANES
        for s in range(x_ref.shape[0] // slab):
            r = pl.ds(s * slab, slab)
            x = x_ref[r, :] + gate * acc_ref[r, :]
            var = jnp.mean(x * x, axis=-1, keepdims=True)
            out_ref[r, :] = x * lax.rsqrt(var + NORM_EPS) * fw


def _ffn(h, x1, gate, final_w, w_in, w_out, seq, tm=512, tf=512):
    m, d = x1.shape
    hidden = w_out.shape[0]
    nj = hidden // tf
    tpb = seq // tm
    row = pl.BlockSpec((tm, d), lambda i, j: (i, 0))
    return pl.pallas_call(
        functools.partial(_ffn_kernel, nj=nj),
        grid=(m // tm, nj),
        in_specs=[row, row,
                  pl.BlockSpec((1, 1, d), lambda i, j: (i // tpb, 0, 0)),
                  pl.BlockSpec((1, d), lambda i, j: (0, 0)),
                  pl.BlockSpec((d, tf), lambda i, j: (0, j)),
                  pl.BlockSpec((d, tf), lambda i, j: (0, j + nj)),
                  pl.BlockSpec((tf, d), lambda i, j: (j, 0))],
        out_specs=row,
        out_shape=jax.ShapeDtypeStruct((m, d), F32),
        scratch_shapes=[pltpu.VMEM((tm, d), F32)],
        compiler_params=pltpu.CompilerParams(
            dimension_semantics=("arbitrary", "arbitrary"), vmem_limit_bytes=VMEM_LIMIT),
        name="ffn",
    )(h, x1, gate, final_w.reshape(1, d), w_in, w_in, w_out)


def kernel(x, c, w_ada, b_ada, norm_mix_w, w_in, w_gla_gate, b_gla_gate, gla_norm_w,
           hg_lower_bound_logits, hg_norm_w, w_out, norm_ffn_w, w_ffn_in, w_ffn_out,
           final_norm_w):
    batch, seq, d = x.shape
    assert w_ada.shape[0] == 1, "single-layer block"
    dv_g = gla_norm_w.shape[1]
    gla_k = GLA_HEADS * HG_EXPAND
    lr_end = 2 * gla_k + 2 * GLA_HEADS * dv_g + GLA_GATE_RANK
    x2d = x.reshape(batch * seq, d)

    mod = _ada(c, w_ada[0], b_ada[0])
    shift_m, scale_m, gate_m, shift_f, scale_f, gate_f = [
        mod[:, i * d:(i + 1) * d].reshape(batch, 1, d) for i in range(6)]

    pad = LANES - GLA_GATE_RANK
    w_in_pad = _wprep(jnp.transpose(w_in[0]), lr_end, pad)
    w_gate_pad = jnp.concatenate(
        [w_gla_gate[0], jnp.zeros((pad, gla_k), w_gla_gate.dtype)], axis=0).astype(BF16)

    p = _inproj(x2d, norm_mix_w[0], scale_m, shift_m, w_in_pad, seq)
    o = _mixer(p, w_gate_pad, b_gla_gate[0], hg_lower_bound_logits, gla_norm_w[0],
               hg_norm_w[0], batch, seq)
    x1, h2 = _outproj(o, w_out[0].astype(BF16), x2d, gate_m, norm_ffn_w[0], scale_f, shift_f, seq)
    out = _ffn(h2, x1, gate_f, final_norm_w,
               w_ffn_in[0].astype(BF16), w_ffn_out[0].astype(BF16), seq)
    return out.reshape(batch, seq, d)
```

```python
import functools

import numpy as np
import jax
import jax.numpy as jnp
from jax import lax
from jax.experimental import pallas as pl
from jax.experimental.pallas import tpu as pltpu

F32 = jnp.float32
BF16 = jnp.bfloat16

NORM_EPS = 1e-6
LOG2_E = 1.4426950408889634
GLA_HEADS = 4
GLA_GATE_RANK = 16
GLA_GATE_NORMALIZER = 16.0
HG_EXPAND = 128
LANES = 128
SUBLANES = 8
MXU_COLS = 256
MIX_CHUNK = 128
VMEM_LIMIT = 56 * 1024 * 1024


def _mm(a, b):
    return jnp.dot(a, b, preferred_element_type=F32)


def _mm_nt(a, b):
    return lax.dot_general(a, b, (((1,), (1,)), ((), ())), preferred_element_type=F32)


def _mm_tn(a, b):
    return lax.dot_general(a, b, (((0,), (0,)), ((), ())), preferred_element_type=F32)


def _silu(x):
    return x * jax.nn.sigmoid(x)


def _rms(x, w):
    var = jnp.mean(x * x, axis=-1, keepdims=True)
    return x * lax.rsqrt(var + NORM_EPS) * w


def _ada_kernel(cb_ref, w_ref, b_ref, o_ref, act_ref):
    nb = cb_ref.shape[0]
    tn = w_ref.shape[1]

    @pl.when(pl.program_id(0) == 0)
    def _():
        act_ref[...] = _silu(cb_ref[...])

    for n in range(tn // LANES):
        w = w_ref[:, n * LANES:(n + 1) * LANES]
        for b in range(nb):
            o_ref[b:b + 1, n * LANES:(n + 1) * LANES] = (
                jnp.sum(w * act_ref[b], axis=0, keepdims=True)
                + b_ref[:, n * LANES:(n + 1) * LANES])


def _ada(c, w_ada, b_ada, tn=1024):
    nb, d = c.shape
    n = w_ada.shape[1]
    cb = jnp.broadcast_to(c[:, :, None], (nb, d, LANES))
    return pl.pallas_call(
        _ada_kernel,
        grid=(n // tn,),
        in_specs=[pl.BlockSpec((nb, d, LANES), lambda j: (0, 0, 0)),
                  pl.BlockSpec((d, tn), lambda j: (0, j)),
                  pl.BlockSpec((1, tn), lambda j: (0, j))],
        out_specs=pl.BlockSpec((nb, tn), lambda j: (0, j)),
        out_shape=jax.ShapeDtypeStruct((nb, n), F32),
        scratch_shapes=[pltpu.VMEM((nb, d, LANES), F32)],
        compiler_params=pltpu.CompilerParams(
            dimension_semantics=("arbitrary",), vmem_limit_bytes=VMEM_LIMIT),
        name="ada",
    )(cb, w_ada, b_ada.reshape(1, n))


def _wprep_kernel(wt_ref, o_ref, *, lr_end, pad):
    n_in = wt_ref.shape[0]
    o_ref[:lr_end] = wt_ref[:lr_end].astype(BF16)
    o_ref[lr_end:lr_end + pad] = jnp.zeros((pad, o_ref.shape[1]), BF16)
    o_ref[lr_end + pad:] = wt_ref[lr_end:n_in].astype(BF16)


def _wprep(wt, lr_end, pad, tc=256):
    n_in, d = wt.shape
    return pl.pallas_call(
        functools.partial(_wprep_kernel, lr_end=lr_end, pad=pad),
        grid=(d // tc,),
        in_specs=[pl.BlockSpec((n_in, tc), lambda i: (0, i))],
        out_specs=pl.BlockSpec((n_in + pad, tc), lambda i: (0, i)),
        out_shape=jax.ShapeDtypeStruct((n_in + pad, d), BF16),
        compiler_params=pltpu.CompilerParams(
            dimension_semantics=("arbitrary",), vmem_limit_bytes=VMEM_LIMIT),
        name="wprep",
    )(wt)


def _norm_modulate_rows(dst_ref, x_ref, row0, rows, nw, scale1, shift):
    wmod = nw * scale1
    slab = 2 * SUBLANES
    for s in range(rows // slab):
        r = pl.ds(row0 + s * slab, slab)
        x = x_ref[r, :]
        var = jnp.mean(x * x, axis=-1, keepdims=True)
        dst_ref[r, :] = (x * lax.rsqrt(var + NORM_EPS) * wmod + shift).astype(dst_ref.dtype)


def _inproj_kernel(x0_ref, xn_ref, nw_ref, sc0_ref, sh0_ref, scn_ref, shn_ref, wt_ref, o_ref,
                   h_ref, *, n_tiles):
    i, j = pl.program_id(0), pl.program_id(1)
    tm = x0_ref.shape[0]

    @pl.when((i == 0) & (j == 0))
    def _():
        _norm_modulate_rows(h_ref.at[0], x0_ref, 0, tm, nw_ref[...], 1.0 + sc0_ref[0], sh0_ref[0])

    o_ref[...] = _mm_nt(h_ref[i % 2], wt_ref[...]).astype(o_ref.dtype)

    share = -(-tm // (n_tiles * 32)) * 32
    row0 = pl.multiple_of(jnp.minimum(j * share, tm - share), 32)
    _norm_modulate_rows(h_ref.at[(i + 1) % 2], xn_ref, row0, share, nw_ref[...],
                        1.0 + scn_ref[0], shn_ref[0])


def _inproj(x2d, norm_w, scale, shift, wt, seq, tm=512, n_tiles=3):
    m, d = x2d.shape
    n = wt.shape[0]
    tn = n // n_tiles
    tpb = seq // tm
    nm = m // tm
    nxt = lambda i: jnp.minimum(i + 1, nm - 1)
    return pl.pallas_call(
        functools.partial(_inproj_kernel, n_tiles=n_tiles),
        grid=(nm, n_tiles),
        in_specs=[pl.BlockSpec((tm, d), lambda i, j: (0, 0)),
                  pl.BlockSpec((tm, d), lambda i, j: (nxt(i), 0)),
                  pl.BlockSpec((1, d), lambda i, j: (0, 0)),
                  pl.BlockSpec((1, 1, d), lambda i, j: (0, 0, 0)),
                  pl.BlockSpec((1, 1, d), lambda i, j: (0, 0, 0)),
                  pl.BlockSpec((1, 1, d), lambda i, j: (nxt(i) // tpb, 0, 0)),
                  pl.BlockSpec((1, 1, d), lambda i, j: (nxt(i) // tpb, 0, 0)),
                  pl.BlockSpec((tn, d), lambda i, j: (j, 0))],
        out_specs=pl.BlockSpec((tm, tn), lambda i, j: (i, j)),
        out_shape=jax.ShapeDtypeStruct((m, n), BF16),
        scratch_shapes=[pltpu.VMEM((2, tm, d), BF16)],
        compiler_params=pltpu.CompilerParams(
            dimension_semantics=("arbitrary", "arbitrary"), vmem_limit_bytes=VMEM_LIMIT),
        name="inproj",
    )(x2d, x2d, norm_w.reshape(1, d), scale, shift, scale, shift, wt)


def _small_levels(c):
    return [h for h in (SUBLANES // 2, SUBLANES // 4, SUBLANES // 8) if 1 <= h < c]


def _big_levels(c):
    out, h = [], c // 2
    while h >= SUBLANES:
        out.append(h)
        h //= 2
    return out


def _cumsum_matrix(c):
    t = np.arange(c)[:, None]
    s = np.arange(c)[None, :]
    blocks = [s <= t]
    for h in _small_levels(c):
        mid = (t // (2 * h)) * (2 * h) + h - 1
        second = (t % (2 * h)) >= h
        blocks.append(np.where(second, (s > mid) & (s <= t), (s > t) & (s <= mid)))
    return np.concatenate(blocks, axis=0).astype(np.float32)


def _level_masks(c):
    i = np.arange(c)[:, None]
    j = np.arange(c)[None, :]
    masks = [i == j]
    for h in _big_levels(c) + _small_levels(c):
        same = (i // (2 * h)) == (j // (2 * h))
        masks.append(same & ((i % (2 * h)) >= h) & ((j % (2 * h)) < h))
    m = np.stack(masks).astype(np.float32)
    return np.concatenate([m, m], axis=2)


def _mixer_kernel(p_ref, wcs_ref, mask_ref, wgate_ref, bgate_ref, lbl_ref,
                  gnw_ref, hnw_ref, o_ref, sg_ref, sh_ref, *, gla_scale):
    c = p_ref.shape[0]
    kd = HG_EXPAND
    n_gla, dv_g = sg_ref.shape[0], sg_ref.shape[1]
    n_hg, dv_h = sh_ref.shape[0], sh_ref.shape[1]
    gla_k, gla_v, hg_w = n_gla * kd, n_gla * dv_g, n_hg * kd
    o_gq, o_gk, o_gv, o_gg = 0, gla_k, 2 * gla_k, 2 * gla_k + gla_v
    o_lr = o_gg + gla_v
    o_hq = o_lr + LANES
    o_hf = o_hq + hg_w
    o_hi = o_hf + hg_w
    o_hg = o_hi + hg_w
    big = _big_levels(c)
    small = _small_levels(c)

    @pl.when(pl.program_id(1) == 0)
    def _():
        sg_ref[...] = jnp.zeros_like(sg_ref)
        sh_ref[...] = jnp.zeros_like(sh_ref)

    z = _mm(p_ref[:, o_lr:o_lr + LANES], wgate_ref[...]) + bgate_ref[...]
    g_gla = (jnp.minimum(z, 0.0) - jnp.log1p(jnp.exp(-jnp.abs(z)))) * (LOG2_E / GLA_GATE_NORMALIZER)
    lbl = lbl_ref[...]
    lbe = jnp.exp(lbl - jnp.max(lbl, axis=0, keepdims=True))
    lb = lbe[0:1, :] / jnp.sum(lbe, axis=0, keepdims=True)
    f = lb + (1.0 - lb) * jax.nn.sigmoid(p_ref[:, o_hf:o_hf + hg_w].astype(F32))
    g_hg = jnp.log2(f)
    k_hg = (1.0 - f).astype(BF16)

    wcs = wcs_ref[...]
    zero = jnp.zeros((kd, c), BF16)

    def blockdiag_t(kk):
        return jnp.concatenate(
            [jnp.concatenate([kk[:, :kd].T, zero], axis=1),
             jnp.concatenate([zero, kk[:, kd:].T], axis=1)], axis=0)

    def decays(g2):
        g_hi = g2.astype(BF16)
        g_lo = (g2 - g_hi.astype(F32)).astype(BF16)
        cs = _mm(wcs, g_hi)
        b = cs[0:c] + _mm(wcs[0:c], g_lo)
        exps = []
        for h in big:
            parts = []
            for s in range(0, c, 2 * h):
                mid = jnp.broadcast_to(b[s + h - 1:s + h, :], (h, 2 * kd))
                parts += [mid - b[s:s + h], b[s + h:s + 2 * h] - mid]
            exps.append(jnp.exp2(jnp.concatenate(parts, axis=0)).astype(BF16))
        for i in range(len(small)):
            exps.append(jnp.exp2(cs[(1 + i) * c:(2 + i) * c]).astype(BF16))
        e_b = jnp.exp2(b)
        e_r = jnp.exp2(jnp.broadcast_to(b[c - 1:c, :], (c, 2 * kd)) - b)
        return exps, e_b, e_r

    def scores(qb, kb, exps):
        s = jnp.where(mask_ref[0] > 0, _mm(qb, blockdiag_t(kb)).astype(BF16), 0.0)
        for lv, eb in enumerate(exps):
            s = jnp.where(mask_ref[1 + lv] > 0,
                          _mm(qb * eb, blockdiag_t(kb * eb)).astype(BF16), s)
        return s

    def outputs(s, qb, kb, e_b, e_r, vs, gates, st_ref, heads, nw, out_off, dv, q_scale):
        qe = qb * e_b.astype(BF16)
        ke = kb * e_r.astype(BF16)
        for i, hd in enumerate(heads):
            sl = slice(i * kd, (i + 1) * kd)
            st = st_ref[hd]
            o = _mm(s[:, i * c:(i + 1) * c], vs[i]) + _mm_nt(qe[:, sl], st.astype(BF16))
            st_ref[hd] = st * e_b[c - 1:c, sl] + _mm_tn(vs[i], ke[:, sl])
            var = jnp.mean(o * o, axis=-1, keepdims=True)
            r = q_scale * lax.rsqrt((q_scale * q_scale) * var + NORM_EPS)
            on = o * r * nw * _silu(gates[i].astype(F32))
            o_ref[:, out_off + hd * dv:out_off + (hd + 1) * dv] = on.astype(o_ref.dtype)

    pairs = []
    for p in range(n_gla // 2):
        lo = 2 * p * kd
        heads = (2 * p, 2 * p + 1)
        pairs.append(dict(
            g2=g_gla[:, lo:lo + 2 * kd],
            qb=p_ref[:, o_gq + lo:o_gq + lo + 2 * kd],
            kb=p_ref[:, o_gk + lo:o_gk + lo + 2 * kd],
            vs=[p_ref[:, o_gv + hd * dv_g:o_gv + (hd + 1) * dv_g] for hd in heads],
            gates=[p_ref[:, o_gg + hd * dv_g:o_gg + (hd + 1) * dv_g] for hd in heads],
            tail=(sg_ref, heads, gnw_ref[...], 0, dv_g, gla_scale)))
    for p in range(n_hg // 2):
        lo = 2 * p * kd
        heads = (2 * p, 2 * p + 1)
        pairs.append(dict(
            g2=g_hg[:, lo:lo + 2 * kd],
            qb=_silu(p_ref[:, o_hq + lo:o_hq + lo + 2 * kd].astype(F32)).astype(BF16),
            kb=k_hg[:, lo:lo + 2 * kd],
            vs=[p_ref[:, o_hi + hd * dv_h:o_hi + (hd + 1) * dv_h] for hd in heads],
            gates=[p_ref[:, o_hg + hd * dv_h:o_hg + (hd + 1) * dv_h] for hd in heads],
            tail=(sh_ref, heads, hnw_ref[...], gla_v, dv_h, 1.0)))
    dec = [decays(pr["g2"]) for pr in pairs]
    sc = [scores(pr["qb"], pr["kb"], d[0]) for pr, d in zip(pairs, dec)]
    for pr, d, s in zip(pairs, dec, sc):
        outputs(s, pr["qb"], pr["kb"], d[1], d[2], pr["vs"], pr["gates"], *pr["tail"])


def _mixer(p, w_gate_pad, b_gate, lb_logits, gla_norm_w, hg_norm_w, batch, seq):
    m = p.shape[0]
    c = MIX_CHUNK
    steps = seq // c
    wcs = jnp.asarray(_cumsum_matrix(c), BF16)
    masks = jnp.asarray(_level_masks(c), BF16)
    dv_g = gla_norm_w.shape[0]
    dv_h = hg_norm_w.shape[0]
    n_hg = lb_logits.shape[1] // HG_EXPAND
    width = GLA_HEADS * dv_g + n_hg * dv_h
    const2 = lambda b, t: (0, 0)
    return pl.pallas_call(
        functools.partial(_mixer_kernel, gla_scale=float(HG_EXPAND) ** -0.5),
        grid=(batch, steps),
        in_specs=[pl.BlockSpec((c, p.shape[1]), lambda b, t: (b * steps + t, 0)),
                  pl.BlockSpec(wcs.shape, const2),
                  pl.BlockSpec(masks.shape, lambda b, t: (0, 0, 0)),
                  pl.BlockSpec(w_gate_pad.shape, const2),
                  pl.BlockSpec((1, b_gate.shape[0]), const2),
                  pl.BlockSpec(lb_logits.shape, const2),
                  pl.BlockSpec((1, dv_g), const2),
                  pl.BlockSpec((1, dv_h), const2)],
        out_specs=pl.BlockSpec((c, width), lambda b, t: (b * steps + t, 0)),
        out_shape=jax.ShapeDtypeStruct((m, width), BF16),
        scratch_shapes=[pltpu.VMEM((GLA_HEADS, dv_g, HG_EXPAND), F32),
                        pltpu.VMEM((n_hg, dv_h, HG_EXPAND), F32)],
        compiler_params=pltpu.CompilerParams(
            dimension_semantics=("arbitrary", "arbitrary"), vmem_limit_bytes=VMEM_LIMIT),
        name="mixer",
    )(p, wcs, masks, w_gate_pad, b_gate.reshape(1, -1), lb_logits,
      gla_norm_w.reshape(1, -1), hg_norm_w.reshape(1, -1))


def _outproj_kernel(o_ref, w_ref, x_ref, gate_ref, nw_ref, sc_ref, sh_ref, x1_ref, h_ref):
    x1_ref[...] = x_ref[...] + gate_ref[0] * _mm(o_ref[...], w_ref[...])
    _norm_modulate_rows(h_ref, x1_ref, 0, x1_ref.shape[0], nw_ref[...], 1.0 + sc_ref[0], sh_ref[0])


def _outproj(o, w_out, x2d, gate, norm_w, scale, shift, seq, tm=512):
    m, d = x2d.shape
    tpb = seq // tm
    row = pl.BlockSpec((tm, d), lambda i: (i, 0))
    mod = pl.BlockSpec((1, 1, d), lambda i: (i // tpb, 0, 0))
    return pl.pallas_call(
        _outproj_kernel,
        grid=(m // tm,),
        in_specs=[pl.BlockSpec((tm, o.shape[1]), lambda i: (i, 0)),
                  pl.BlockSpec(w_out.shape, lambda i: (0, 0)),
                  row, mod, pl.BlockSpec((1, d), lambda i: (0, 0)), mod, mod],
        out_specs=[row, row],
        out_shape=[jax.ShapeDtypeStruct((m, d), F32), jax.ShapeDtypeStruct((m, d), BF16)],
        compiler_params=pltpu.CompilerParams(
            dimension_semantics=("arbitrary",), vmem_limit_bytes=VMEM_LIMIT),
        name="outproj",
    )(o, w_out, x2d, gate, norm_w.reshape(1, d), scale, shift)


def _ffn_kernel(h_ref, x_ref, gate_ref, fw_ref, wau_ref, wo_ref, out_ref, acc_ref, *, nj):
    j = pl.program_id(1)
    tf = wo_ref.shape[0]

    @pl.when(j == 0)
    def _():
        acc_ref[...] = jnp.zeros_like(acc_ref)

    h = h_ref[...]
    act = _silu(_mm(h, wau_ref[0, :, :tf])) * _mm(h, wau_ref[0, :, tf:])
    acc_ref[...] += _mm(act.astype(BF16), wo_ref[...])

    @pl.when(j == nj - 1)
    def _():
        gate, fw = gate_ref[0], fw_ref[...]
        slab = 2 * SUBLANES
        for s in range(x_ref.shape[0] // slab):
            r = pl.ds(s * slab, slab)
            x = x_ref[r, :] + gate * acc_ref[r, :]
            var = jnp.mean(x * x, axis=-1, keepdims=True)
            out_ref[r, :] = x * lax.rsqrt(var + NORM_EPS) * fw


def _ffn_in_tiles_kernel(wa_ref, wu_ref, o_ref):
    tf = wa_ref.shape[1]
    o_ref[0, :, :tf] = wa_ref[...].astype(o_ref.dtype)
    o_ref[0, :, tf:] = wu_ref[...].astype(o_ref.dtype)


def _ffn_in_tiles(w_in, nj, tf):
    d = w_in.shape[0]
    return pl.pallas_call(
        _ffn_in_tiles_kernel,
        grid=(nj,),
        in_specs=[pl.BlockSpec((d, tf), lambda j: (0, j)),
                  pl.BlockSpec((d, tf), lambda j: (0, j + nj))],
        out_specs=pl.BlockSpec((1, d, 2 * tf), lambda j: (j, 0, 0)),
        out_shape=jax.ShapeDtypeStruct((nj, d, 2 * tf), BF16),
        compiler_params=pltpu.CompilerParams(
            dimension_semantics=("arbitrary",), vmem_limit_bytes=VMEM_LIMIT),
        name="ffn_in_tiles",
    )(w_in, w_in)


def _ffn(h, x1, gate, final_w, w_in, w_out, seq, tm=512, tf=512):
    m, d = x1.shape
    hidden = w_out.shape[0]
    nj = hidden // tf
    tpb = seq // tm
    w_au = _ffn_in_tiles(w_in, nj, tf)
    row = pl.BlockSpec((tm, d), lambda i, j: (i, 0))
    return pl.pallas_call(
        functools.partial(_ffn_kernel, nj=nj),
        grid=(m // tm, nj),
        in_specs=[row, row,
                  pl.BlockSpec((1, 1, d), lambda i, j: (i // tpb, 0, 0)),
                  pl.BlockSpec((1, d), lambda i, j: (0, 0)),
                  pl.BlockSpec((1, d, 2 * tf), lambda i, j: (j, 0, 0)),
                  pl.BlockSpec((tf, d), lambda i, j: (j, 0))],
        out_specs=row,
        out_shape=jax.ShapeDtypeStruct((m, d), F32),
        scratch_shapes=[pltpu.VMEM((tm, d), F32)],
        compiler_params=pltpu.CompilerParams(
            dimension_semantics=("arbitrary", "arbitrary"), vmem_limit_bytes=VMEM_LIMIT),
        name="ffn",
    )(h, x1, gate, final_w.reshape(1, d), w_au, w_out)


def kernel(x, c, w_ada, b_ada, norm_mix_w, w_in, w_gla_gate, b_gla_gate, gla_norm_w,
           hg_lower_bound_logits, hg_norm_w, w_out, norm_ffn_w, w_ffn_in, w_ffn_out,
           final_norm_w):
    batch, seq, d = x.shape
    assert w_ada.shape[0] == 1, "single-layer block"
    dv_g = gla_norm_w.shape[1]
    gla_k = GLA_HEADS * HG_EXPAND
    lr_end = 2 * gla_k + 2 * GLA_HEADS * dv_g + GLA_GATE_RANK
    x2d = x.reshape(batch * seq, d)

    mod = _ada(c, w_ada[0], b_ada[0])
    shift_m, scale_m, gate_m, shift_f, scale_f, gate_f = [
        mod[:, i * d:(i + 1) * d].reshape(batch, 1, d) for i in range(6)]

    pad = LANES - GLA_GATE_RANK
    w_in_pad = _wprep(jnp.transpose(w_in[0]), lr_end, pad)
    w_gate_pad = jnp.concatenate(
        [w_gla_gate[0], jnp.zeros((pad, gla_k), w_gla_gate.dtype)], axis=0).astype(BF16)

    p = _inproj(x2d, norm_mix_w[0], scale_m, shift_m, w_in_pad, seq)
    o = _mixer(p, w_gate_pad, b_gla_gate[0], hg_lower_bound_logits, gla_norm_w[0],
               hg_norm_w[0], batch, seq)
    x1, h2 = _outproj(o, w_out[0].astype(BF16), x2d, gate_m, norm_ffn_w[0], scale_f, shift_f, seq)
    out = _ffn(h2, x1, gate_f, final_norm_w,
               w_ffn_in[0], w_ffn_out[0].astype(BF16), seq)
    return out.reshape(batch, seq, d)
```

```python
import functools

import numpy as np
import jax
import jax.numpy as jnp
from jax import lax
from jax.experimental import pallas as pl
from jax.experimental.pallas import tpu as pltpu

F32 = jnp.float32
BF16 = jnp.bfloat16

NORM_EPS = 1e-6
LOG2_E = 1.4426950408889634
GLA_HEADS = 4
GLA_GATE_RANK = 16
GLA_GATE_NORMALIZER = 16.0
HG_EXPAND = 128
LANES = 128
SUBLANES = 8
MXU_COLS = 256
MIX_CHUNK = 128
VMEM_LIMIT = 56 * 1024 * 1024


def _mm(a, b):
    return jnp.dot(a, b, preferred_element_type=F32)


def _mm_nt(a, b):
    return lax.dot_general(a, b, (((1,), (1,)), ((), ())), preferred_element_type=F32)


def _mm_tn(a, b):
    return lax.dot_general(a, b, (((0,), (0,)), ((), ())), preferred_element_type=F32)


def _silu(x):
    return x * jax.nn.sigmoid(x)


def _rms(x, w):
    var = jnp.mean(x * x, axis=-1, keepdims=True)
    return x * lax.rsqrt(var + NORM_EPS) * w


def _ada_kernel(cb_ref, w_ref, b_ref, o_ref, act_ref):
    nb = cb_ref.shape[0]
    tn = w_ref.shape[1]

    @pl.when(pl.program_id(0) == 0)
    def _():
        act_ref[...] = _silu(cb_ref[...])

    for n in range(tn // LANES):
        w = w_ref[:, n * LANES:(n + 1) * LANES]
        for b in range(nb):
            o_ref[b:b + 1, n * LANES:(n + 1) * LANES] = (
                jnp.sum(w * act_ref[b], axis=0, keepdims=True)
                + b_ref[:, n * LANES:(n + 1) * LANES])


def _ada(c, w_ada, b_ada, tn=1024):
    nb, d = c.shape
    n = w_ada.shape[1]
    cb = jnp.broadcast_to(c[:, :, None], (nb, d, LANES))
    return pl.pallas_call(
        _ada_kernel,
        grid=(n // tn,),
        in_specs=[pl.BlockSpec((nb, d, LANES), lambda j: (0, 0, 0)),
                  pl.BlockSpec((d, tn), lambda j: (0, j)),
                  pl.BlockSpec((1, tn), lambda j: (0, j))],
        out_specs=pl.BlockSpec((nb, tn), lambda j: (0, j)),
        out_shape=jax.ShapeDtypeStruct((nb, n), F32),
        scratch_shapes=[pltpu.VMEM((nb, d, LANES), F32)],
        compiler_params=pltpu.CompilerParams(
            dimension_semantics=("arbitrary",), vmem_limit_bytes=VMEM_LIMIT),
        name="ada",
    )(cb, w_ada, b_ada.reshape(1, n))


def _wprep_kernel(wt_ref, o_ref, *, lr_end, pad):
    n_in = wt_ref.shape[0]
    o_ref[:lr_end] = wt_ref[:lr_end].astype(BF16)
    o_ref[lr_end:lr_end + pad] = jnp.zeros((pad, o_ref.shape[1]), BF16)
    o_ref[lr_end + pad:] = wt_ref[lr_end:n_in].astype(BF16)


def _wprep(wt, lr_end, pad, tc=256):
    n_in, d = wt.shape
    return pl.pallas_call(
        functools.partial(_wprep_kernel, lr_end=lr_end, pad=pad),
        grid=(d // tc,),
        in_specs=[pl.BlockSpec((n_in, tc), lambda i: (0, i))],
        out_specs=pl.BlockSpec((n_in + pad, tc), lambda i: (0, i)),
        out_shape=jax.ShapeDtypeStruct((n_in + pad, d), BF16),
        compiler_params=pltpu.CompilerParams(
            dimension_semantics=("arbitrary",), vmem_limit_bytes=VMEM_LIMIT),
        name="wprep",
    )(wt)


def _norm_modulate_rows(dst_ref, x_ref, row0, rows, nw, scale1, shift):
    wmod = nw * scale1
    slab = 2 * SUBLANES
    for s in range(rows // slab):
        r = pl.ds(row0 + s * slab, slab)
        x = x_ref[r, :]
        var = jnp.mean(x * x, axis=-1, keepdims=True)
        dst_ref[r, :] = (x * lax.rsqrt(var + NORM_EPS) * wmod + shift).astype(dst_ref.dtype)


def _inproj_kernel(x0_ref, xn_ref, nw_ref, sc0_ref, sh0_ref, scn_ref, shn_ref, wt_ref, o_ref,
                   h_ref, *, n_tiles):
    i, j = pl.program_id(0), pl.program_id(1)
    tm = x0_ref.shape[0]

    @pl.when((i == 0) & (j == 0))
    def _():
        _norm_modulate_rows(h_ref.at[0], x0_ref, 0, tm, nw_ref[...], 1.0 + sc0_ref[0], sh0_ref[0])

    o_ref[...] = _mm_nt(h_ref[i % 2], wt_ref[...]).astype(o_ref.dtype)

    share = -(-tm // (n_tiles * 32)) * 32
    row0 = pl.multiple_of(jnp.minimum(j * share, tm - share), 32)
    _norm_modulate_rows(h_ref.at[(i + 1) % 2], xn_ref, row0, share, nw_ref[...],
                        1.0 + scn_ref[0], shn_ref[0])


def _inproj(x2d, norm_w, scale, shift, wt, seq, tm=512, n_tiles=3):
    m, d = x2d.shape
    n = wt.shape[0]
    tn = n // n_tiles
    tpb = seq // tm
    nm = m // tm
    nxt = lambda i: jnp.minimum(i + 1, nm - 1)
    return pl.pallas_call(
        functools.partial(_inproj_kernel, n_tiles=n_tiles),
        grid=(nm, n_tiles),
        in_specs=[pl.BlockSpec((tm, d), lambda i, j: (0, 0)),
                  pl.BlockSpec((tm, d), lambda i, j: (nxt(i), 0)),
                  pl.BlockSpec((1, d), lambda i, j: (0, 0)),
                  pl.BlockSpec((1, 1, d), lambda i, j: (0, 0, 0)),
                  pl.BlockSpec((1, 1, d), lambda i, j: (0, 0, 0)),
                  pl.BlockSpec((1, 1, d), lambda i, j: (nxt(i) // tpb, 0, 0)),
                  pl.BlockSpec((1, 1, d), lambda i, j: (nxt(i) // tpb, 0, 0)),
                  pl.BlockSpec((tn, d), lambda i, j: (j, 0))],
        out_specs=pl.BlockSpec((tm, tn), lambda i, j: (i, j)),
        out_shape=jax.ShapeDtypeStruct((m, n), BF16),
        scratch_shapes=[pltpu.VMEM((2, tm, d), BF16)],
        compiler_params=pltpu.CompilerParams(
            dimension_semantics=("arbitrary", "arbitrary"), vmem_limit_bytes=VMEM_LIMIT),
        name="inproj",
    )(x2d, x2d, norm_w.reshape(1, d), scale, shift, scale, shift, wt)


def _small_levels(c):
    return [h for h in (SUBLANES // 2, SUBLANES // 4, SUBLANES // 8) if 1 <= h < c]


def _big_levels(c):
    out, h = [], c // 2
    while h >= SUBLANES:
        out.append(h)
        h //= 2
    return out


def _cumsum_matrix(c):
    t = np.arange(c)[:, None]
    s = np.arange(c)[None, :]
    blocks = [s <= t]
    for h in _small_levels(c):
        mid = (t // (2 * h)) * (2 * h) + h - 1
        second = (t % (2 * h)) >= h
        blocks.append(np.where(second, (s > mid) & (s <= t), (s > t) & (s <= mid)))
    return np.concatenate(blocks, axis=0).astype(np.float32)


def _level_masks(c):
    i = np.arange(c)[:, None]
    j = np.arange(c)[None, :]
    masks = [i == j]
    for h in _big_levels(c) + _small_levels(c):
        same = (i // (2 * h)) == (j // (2 * h))
        masks.append(same & ((i % (2 * h)) >= h) & ((j % (2 * h)) < h))
    m = np.stack(masks).astype(np.float32)
    return np.concatenate([m, m], axis=2)


def _mixer_kernel(p_ref, wcs_ref, mask_ref, wgate_ref, bgate_ref, lbl_ref,
                  gnw_ref, hnw_ref, o_ref, sg_ref, sh_ref, *, gla_scale):
    c = MIX_CHUNK
    kd = HG_EXPAND
    n_gla, dv_g = sg_ref.shape[0], sg_ref.shape[1]
    n_hg, dv_h = sh_ref.shape[0], sh_ref.shape[1]
    gla_k, gla_v, hg_w = n_gla * kd, n_gla * dv_g, n_hg * kd
    o_gq, o_gk, o_gv, o_gg = 0, gla_k, 2 * gla_k, 2 * gla_k + gla_v
    o_lr = o_gg + gla_v
    o_hq = o_lr + LANES
    o_hf = o_hq + hg_w
    o_hi = o_hf + hg_w
    o_hg = o_hi + hg_w
    big = _big_levels(c)
    small = _small_levels(c)

    @pl.when(pl.program_id(1) == 0)
    def _():
        sg_ref[...] = jnp.zeros_like(sg_ref)
        sh_ref[...] = jnp.zeros_like(sh_ref)

    lbl = lbl_ref[...]
    lbe = jnp.exp(lbl - jnp.max(lbl, axis=0, keepdims=True))
    lb = lbe[0:1, :] / jnp.sum(lbe, axis=0, keepdims=True)

    def gates_of(rows):
        z = _mm(p_ref[rows, o_lr:o_lr + LANES], wgate_ref[...]) + bgate_ref[...]
        g_gla = ((jnp.minimum(z, 0.0) - jnp.log1p(jnp.exp(-jnp.abs(z))))
                 * (LOG2_E / GLA_GATE_NORMALIZER))
        f = lb + (1.0 - lb) * jax.nn.sigmoid(p_ref[rows, o_hf:o_hf + hg_w].astype(F32))
        return g_gla, jnp.log2(f), (1.0 - f).astype(BF16)

    wcs = wcs_ref[...]
    zero = jnp.zeros((kd, c), BF16)

    def blockdiag_t(kk):
        return jnp.concatenate(
            [jnp.concatenate([kk[:, :kd].T, zero], axis=1),
             jnp.concatenate([zero, kk[:, kd:].T], axis=1)], axis=0)

    def decays(g2):
        g_hi = g2.astype(BF16)
        g_lo = (g2 - g_hi.astype(F32)).astype(BF16)
        cs = _mm(wcs, g_hi)
        b = cs[0:c] + _mm(wcs[0:c], g_lo)
        exps = []
        for h in big:
            parts = []
            for s in range(0, c, 2 * h):
                mid = jnp.broadcast_to(b[s + h - 1:s + h, :], (h, 2 * kd))
                parts += [mid - b[s:s + h], b[s + h:s + 2 * h] - mid]
            exps.append(jnp.exp2(jnp.concatenate(parts, axis=0)).astype(BF16))
        for i in range(len(small)):
            exps.append(jnp.exp2(cs[(1 + i) * c:(2 + i) * c]).astype(BF16))
        e_b = jnp.exp2(b)
        e_r = jnp.exp2(jnp.broadcast_to(b[c - 1:c, :], (c, 2 * kd)) - b)
        return exps, e_b, e_r

    def scores(qb, kb, exps):
        s = jnp.where(mask_ref[0] > 0, _mm(qb, blockdiag_t(kb)).astype(BF16), 0.0)
        for lv, eb in enumerate(exps):
            s = jnp.where(mask_ref[1 + lv] > 0,
                          _mm(qb * eb, blockdiag_t(kb * eb)).astype(BF16), s)
        return s

    def outputs(s, qb, kb, e_b, e_r, vs, gates, rows, st_ref, heads, nw, out_off, dv, q_scale):
        qe = qb * e_b.astype(BF16)
        ke = kb * e_r.astype(BF16)
        for i, hd in enumerate(heads):
            sl = slice(i * kd, (i + 1) * kd)
            st = st_ref[hd]
            o = _mm(s[:, i * c:(i + 1) * c], vs[i]) + _mm_nt(qe[:, sl], st.astype(BF16))
            st_ref[hd] = st * e_b[c - 1:c, sl] + _mm_tn(vs[i], ke[:, sl])
            var = jnp.mean(o * o, axis=-1, keepdims=True)
            r = q_scale * lax.rsqrt((q_scale * q_scale) * var + NORM_EPS)
            on = o * r * nw * _silu(gates[i].astype(F32))
            o_ref[rows, out_off + hd * dv:out_off + (hd + 1) * dv] = on.astype(o_ref.dtype)

    pairs = []
    for ck in range(p_ref.shape[0] // c):
        rows = slice(ck * c, (ck + 1) * c)
        g_gla, g_hg, k_hg = gates_of(rows)
        for p in range(n_gla // 2):
            lo = 2 * p * kd
            heads = (2 * p, 2 * p + 1)
            pairs.append(dict(
                g2=g_gla[:, lo:lo + 2 * kd],
                qb=p_ref[rows, o_gq + lo:o_gq + lo + 2 * kd],
                kb=p_ref[rows, o_gk + lo:o_gk + lo + 2 * kd],
                vs=[p_ref[rows, o_gv + hd * dv_g:o_gv + (hd + 1) * dv_g] for hd in heads],
                gates=[p_ref[rows, o_gg + hd * dv_g:o_gg + (hd + 1) * dv_g] for hd in heads],
                tail=(rows, sg_ref, heads, gnw_ref[...], 0, dv_g, gla_scale)))
        for p in range(n_hg // 2):
            lo = 2 * p * kd
            heads = (2 * p, 2 * p + 1)
            pairs.append(dict(
                g2=g_hg[:, lo:lo + 2 * kd],
                qb=_silu(p_ref[rows, o_hq + lo:o_hq + lo + 2 * kd].astype(F32)).astype(BF16),
                kb=k_hg[:, lo:lo + 2 * kd],
                vs=[p_ref[rows, o_hi + hd * dv_h:o_hi + (hd + 1) * dv_h] for hd in heads],
                gates=[p_ref[rows, o_hg + hd * dv_h:o_hg + (hd + 1) * dv_h] for hd in heads],
                tail=(rows, sh_ref, heads, hnw_ref[...], gla_v, dv_h, 1.0)))
    dec = [decays(pr["g2"]) for pr in pairs]
    sc = [scores(pr["qb"], pr["kb"], d[0]) for pr, d in zip(pairs, dec)]
    for pr, d, s in zip(pairs, dec, sc):
        outputs(s, pr["qb"], pr["kb"], d[1], d[2], pr["vs"], pr["gates"], *pr["tail"])


def _mixer(p, w_gate_pad, b_gate, lb_logits, gla_norm_w, hg_norm_w, batch, seq, chunks=4):
    m = p.shape[0]
    c = MIX_CHUNK
    rows = chunks * c
    steps = seq // rows
    wcs = jnp.asarray(_cumsum_matrix(c), BF16)
    masks = jnp.asarray(_level_masks(c), BF16)
    dv_g = gla_norm_w.shape[0]
    dv_h = hg_norm_w.shape[0]
    n_hg = lb_logits.shape[1] // HG_EXPAND
    width = GLA_HEADS * dv_g + n_hg * dv_h
    const2 = lambda b, t: (0, 0)
    return pl.pallas_call(
        functools.partial(_mixer_kernel, gla_scale=float(HG_EXPAND) ** -0.5),
        grid=(batch, steps),
        in_specs=[pl.BlockSpec((rows, p.shape[1]), lambda b, t: (b * steps + t, 0)),
                  pl.BlockSpec(wcs.shape, const2),
                  pl.BlockSpec(masks.shape, lambda b, t: (0, 0, 0)),
                  pl.BlockSpec(w_gate_pad.shape, const2),
                  pl.BlockSpec((1, b_gate.shape[0]), const2),
                  pl.BlockSpec(lb_logits.shape, const2),
                  pl.BlockSpec((1, dv_g), const2),
                  pl.BlockSpec((1, dv_h), const2)],
        out_specs=pl.BlockSpec((rows, width), lambda b, t: (b * steps + t, 0)),
        out_shape=jax.ShapeDtypeStruct((m, width), BF16),
        scratch_shapes=[pltpu.VMEM((GLA_HEADS, dv_g, HG_EXPAND), F32),
                        pltpu.VMEM((n_hg, dv_h, HG_EXPAND), F32)],
        compiler_params=pltpu.CompilerParams(
            dimension_semantics=("arbitrary", "arbitrary"), vmem_limit_bytes=VMEM_LIMIT),
        name="mixer",
    )(p, wcs, masks, w_gate_pad, b_gate.reshape(1, -1), lb_logits,
      gla_norm_w.reshape(1, -1), hg_norm_w.reshape(1, -1))


def _outproj_kernel(o_ref, w_ref, x_ref, gate_ref, nw_ref, sc_ref, sh_ref, x1_ref, h_ref):
    x1_ref[...] = x_ref[...] + gate_ref[0] * _mm(o_ref[...], w_ref[...])
    _norm_modulate_rows(h_ref, x1_ref, 0, x1_ref.shape[0], nw_ref[...], 1.0 + sc_ref[0], sh_ref[0])


def _outproj(o, w_out, x2d, gate, norm_w, scale, shift, seq, tm=512):
    m, d = x2d.shape
    tpb = seq // tm
    row = pl.BlockSpec((tm, d), lambda i: (i, 0))
    mod = pl.BlockSpec((1, 1, d), lambda i: (i // tpb, 0, 0))
    return pl.pallas_call(
        _outproj_kernel,
        grid=(m // tm,),
        in_specs=[pl.BlockSpec((tm, o.shape[1]), lambda i: (i, 0)),
                  pl.BlockSpec(w_out.shape, lambda i: (0, 0)),
                  row, mod, pl.BlockSpec((1, d), lambda i: (0, 0)), mod, mod],
        out_specs=[row, row],
        out_shape=[jax.ShapeDtypeStruct((m, d), F32), jax.ShapeDtypeStruct((m, d), BF16)],
        compiler_params=pltpu.CompilerParams(
            dimension_semantics=("arbitrary",), vmem_limit_bytes=VMEM_LIMIT),
        name="outproj",
    )(o, w_out, x2d, gate, norm_w.reshape(1, d), scale, shift)


def _ffn_kernel(h_ref, x_ref, gate_ref, fw_ref, wau_ref, wo_ref, out_ref, acc_ref, *, nj):
    j = pl.program_id(1)
    tf = wo_ref.shape[0]

    @pl.when(j == 0)
    def _():
        acc_ref[...] = jnp.zeros_like(acc_ref)

    h = h_ref[...]
    act = _silu(_mm(h, wau_ref[0, :, :tf])) * _mm(h, wau_ref[0, :, tf:])
    acc_ref[...] += _mm(act.astype(BF16), wo_ref[...])

    @pl.when(j == nj - 1)
    def _():
        gate, fw = gate_ref[0], fw_ref[...]
        slab = 2 * SUBLANES
        for s in range(x_ref.shape[0] // slab):
            r = pl.ds(s * slab, slab)
            x = x_ref[r, :] + gate * acc_ref[r, :]
            var = jnp.mean(x * x, axis=-1, keepdims=True)
            out_ref[r, :] = x * lax.rsqrt(var + NORM_EPS) * fw


def _ffn_in_tiles_kernel(wa_ref, wu_ref, o_ref):
    tf = wa_ref.shape[1]
    o_ref[0, :, :tf] = wa_ref[...].astype(o_ref.dtype)
    o_ref[0, :, tf:] = wu_ref[...].astype(o_ref.dtype)


def _ffn_in_tiles(w_in, nj, tf):
    d = w_in.shape[0]
    return pl.pallas_call(
        _ffn_in_tiles_kernel,
        grid=(nj,),
        in_specs=[pl.BlockSpec((d, tf), lambda j: (0, j)),
                  pl.BlockSpec((d, tf), lambda j: (0, j + nj))],
        out_specs=pl.BlockSpec((1, d, 2 * tf), lambda j: (j, 0, 0)),
        out_shape=jax.ShapeDtypeStruct((nj, d, 2 * tf), BF16),
        compiler_params=pltpu.CompilerParams(
            dimension_semantics=("arbitrary",), vmem_limit_bytes=VMEM_LIMIT),
        name="ffn_in_tiles",
    )(w_in, w_in)


def _ffn(h, x1, gate, final_w, w_in, w_out, seq, tm=512, tf=512):
    m, d = x1.shape
    hidden = w_out.shape[0]
    nj = hidden // tf
    tpb = seq // tm
    w_au = _ffn_in_tiles(w_in, nj, tf)
    row = pl.BlockSpec((tm, d), lambda i, j: (i, 0))
    return pl.pallas_call(
        functools.partial(_ffn_kernel, nj=nj),
        grid=(m // tm, nj),
        in_specs=[row, row,
                  pl.BlockSpec((1, 1, d), lambda i, j: (i // tpb, 0, 0)),
                  pl.BlockSpec((1, d), lambda i, j: (0, 0)),
                  pl.BlockSpec((1, d, 2 * tf), lambda i, j: (j, 0, 0)),
                  pl.BlockSpec((tf, d), lambda i, j: (j, 0))],
        out_specs=row,
        out_shape=jax.ShapeDtypeStruct((m, d), F32),
        scratch_shapes=[pltpu.VMEM((tm, d), F32)],
        compiler_params=pltpu.CompilerParams(
            dimension_semantics=("arbitrary", "arbitrary"), vmem_limit_bytes=VMEM_LIMIT),
        name="ffn",
    )(h, x1, gate, final_w.reshape(1, d), w_au, w_out)


def kernel(x, c, w_ada, b_ada, norm_mix_w, w_in, w_gla_gate, b_gla_gate, gla_norm_w,
           hg_lower_bound_logits, hg_norm_w, w_out, norm_ffn_w, w_ffn_in, w_ffn_out,
           final_norm_w):
    batch, seq, d = x.shape
    assert w_ada.shape[0] == 1, "single-layer block"
    dv_g = gla_norm_w.shape[1]
    gla_k = GLA_HEADS * HG_EXPAND
    lr_end = 2 * gla_k + 2 * GLA_HEADS * dv_g + GLA_GATE_RANK
    x2d = x.reshape(batch * seq, d)

    mod = _ada(c, w_ada[0], b_ada[0])
    shift_m, scale_m, gate_m, shift_f, scale_f, gate_f = [
        mod[:, i * d:(i + 1) * d].reshape(batch, 1, d) for i in range(6)]

    pad = LANES - GLA_GATE_RANK
    w_in_pad = _wprep(jnp.transpose(w_in[0]), lr_end, pad)
    w_gate_pad = jnp.concatenate(
        [w_gla_gate[0], jnp.zeros((pad, gla_k), w_gla_gate.dtype)], axis=0).astype(BF16)

    p = _inproj(x2d, norm_mix_w[0], scale_m, shift_m, w_in_pad, seq)
    o = _mixer(p, w_gate_pad, b_gla_gate[0], hg_lower_bound_logits, gla_norm_w[0],
               hg_norm_w[0], batch, seq)
    x1, h2 = _outproj(o, w_out[0].astype(BF16), x2d, gate_m, norm_ffn_w[0], scale_f, shift_f, seq)
    out = _ffn(h2, x1, gate_f, final_norm_w,
               w_ffn_in[0], w_ffn_out[0].astype(BF16), seq)
    return out.reshape(batch, seq, d)
```
